```python
import math
import jax, jax.numpy as jnp
from jax import lax
import numpy as np

D_MODEL = 1024
BATCH = 2
SEQ = 8192
DEPTH = 4

N_MEM = 256
D_FF = 2816
EPS = 1e-6
ROPE_THETA = 10000.0
GLA_HEADS = 4
GLA_DK = 64
GLA_DV = 128
GLA_RANK = 16
GLA_TAU = 16.0
GLA_CHUNK = 64
GLA_WIDTH = GLA_HEADS * GLA_DV
S5_GROUP = 16
S5_WIDTH = D_MODEL - GLA_WIDTH
S5_GROUPS = S5_WIDTH // S5_GROUP
S5_STATE = 64
S5_DT_MIN = 0.001
S5_DT_MAX = 0.1
ATT_HEADS = 8
ATT_KV_HEADS = 2
ATT_HEAD_DIM = D_MODEL // ATT_HEADS
IDX_HEADS = 8
IDX_DIM = 64
TOPK_MAX = 256
Q_BLOCK = 128
XA_HEADS = 4
XA_HEAD_DIM = D_MODEL // XA_HEADS

N_EVEN = (DEPTH + 1) // 2
N_ODD = DEPTH // 2
EV_SIZES = (GLA_HEADS * GLA_DK, GLA_HEADS * GLA_DK, GLA_WIDTH, GLA_WIDTH, GLA_RANK, S5_WIDTH)
OD_SIZES = (ATT_HEADS * ATT_HEAD_DIM, ATT_KV_HEADS * ATT_HEAD_DIM, ATT_KV_HEADS * ATT_HEAD_DIM,
            IDX_HEADS * IDX_DIM, IDX_DIM, IDX_HEADS)
EV_IN = sum(EV_SIZES)
OD_IN = sum(OD_SIZES)

kernel_name = "hybrid_gla_s5_dsa_macaron_memxattn"


def _splits(sizes):
    out, acc = [], 0
    for s in sizes[:-1]:
        acc += s
        out.append(acc)
    return out


def rmsnorm(x, g):
    xf = x.astype(jnp.float32)
    y = xf * lax.rsqrt(jnp.mean(xf * xf, axis=-1, keepdims=True) + EPS)
    return (y * g.astype(jnp.float32)).astype(x.dtype)


def rope_tables(positions, dim):
    inv = ROPE_THETA ** (-jnp.arange(0, dim, 2, dtype=jnp.float32) / dim)
    ang = positions.astype(jnp.float32)[..., None] * inv
    cos = jnp.cos(ang)
    sin = jnp.sin(ang)
    return jnp.concatenate([cos, cos], -1), jnp.concatenate([sin, sin], -1)


def apply_rope(x, cos, sin):
    xf = x.astype(jnp.float32)
    half = xf.shape[-1] // 2
    rot = jnp.concatenate([-xf[..., half:], xf[..., :half]], -1)
    return (xf * cos[:, :, None, :] + rot * sin[:, :, None, :]).astype(x.dtype)


def swiglu(h, w_gate, w_up, w_down):
    return (jax.nn.silu(h @ w_gate) * (h @ w_up)) @ w_down


def gla_mixer(q, k, v, r, a_low, w_a2, b_a, g_norm):
    f32 = jnp.float32
    Bsz, L, _ = q.shape
    C = GLA_CHUNK
    N = L // C
    log_a = jax.nn.log_sigmoid((a_low @ w_a2 + b_a).astype(f32)) / GLA_TAU

    def chunks(t, d):
        return t.reshape(Bsz, N, C, GLA_HEADS, d)

    qc = chunks(q.astype(f32), GLA_DK) * GLA_DK ** -0.5
    kc = chunks(k.astype(f32), GLA_DK)
    vc = chunks(v.astype(f32), GLA_DV)
    bc = jnp.cumsum(chunks(log_a, GLA_DK), axis=2)
    b_last = bc[:, :, -1]
    q_dec = qc * jnp.exp(bc)
    k_inv = kc * jnp.exp(-bc)
    k_end = kc * jnp.exp(b_last[:, :, None] - bc)
    causal = jnp.tril(jnp.ones((C, C), dtype=bool))
    scores = jnp.where(causal, jnp.einsum('bnihd,bnjhd->bnhij', q_dec, k_inv), 0.0)
    o_intra = jnp.einsum('bnhij,bnjhe->bnihe', scores, vc)
    dS = jnp.einsum('bnjhd,bnjhe->nbhde', k_end, vc)
    decay = jnp.exp(b_last).transpose(1, 0, 2, 3)

    def step(S, inp):
        dec, ds = inp
        return dec[..., None] * S + ds, S

    S0 = jnp.zeros((Bsz, GLA_HEADS, GLA_DK, GLA_DV), f32)
    _, S_prev = lax.scan(step, S0, (decay, dS))
    o_inter = jnp.einsum('bnihd,nbhde->bnihe', q_dec, S_prev)
    o = (o_intra + o_inter).reshape(Bsz, L, GLA_HEADS, GLA_DV)
    o = o * lax.rsqrt(jnp.mean(o * o, axis=-1, keepdims=True) + EPS) * g_norm.astype(f32)
    o = o.reshape(Bsz, L, GLA_WIDTH) * jax.nn.silu(r.astype(f32))
    return o.astype(q.dtype)


def s5_mixer(u, lam_re, lam_im, log_dt, B_re, B_im, C_re, C_im, D, w_glu, b_glu):
    f32 = jnp.float32
    Bsz, L, _ = u.shape
    uf = u.astype(f32).reshape(Bsz, L, S5_GROUPS, S5_GROUP)
    lr = jnp.minimum(lam_re.astype(f32), -1e-4)
    li = lam_im.astype(f32)
    dt = jnp.exp(log_dt.astype(f32))[:, None]
    mag = jnp.exp(lr * dt)
    ab_re = mag * jnp.cos(li * dt)
    ab_im = mag * jnp.sin(li * dt)
    den = lr * lr + li * li
    nr = ab_re - 1.0
    f_re = (nr * lr + ab_im * li) / den
    f_im = (ab_im * lr - nr * li) / den
    Br = B_re.astype(f32)
    Bi = B_im.astype(f32)
    bb_re = f_re[..., None] * Br - f_im[..., None] * Bi
    bb_im = f_re[..., None] * Bi + f_im[..., None] * Br
    bu_re = jnp.einsum('gpc,blgc->blgp', bb_re, uf)
    bu_im = jnp.einsum('gpc,blgc->blgp', bb_im, uf)
    a_re = jnp.broadcast_to(ab_re, bu_re.shape)
    a_im = jnp.broadcast_to(ab_im, bu_im.shape)

    def combine(e1, e2):
        a1r, a1i, b1r, b1i = e1
        a2r, a2i, b2r, b2i = e2
        return (a1r * a2r - a1i * a2i,
                a1r * a2i + a1i * a2r,
                a2r * b1r - a2i * b1i + b2r,
                a2r * b1i + a2i * b1r + b2i)

    _, _, xr, xi = lax.associative_scan(combine, (a_re, a_im, bu_re, bu_im), axis=1)
    y = (jnp.einsum('gcp,blgp->blgc', C_re.astype(f32), xr)
         - jnp.einsum('gcp,blgp->blgc', C_im.astype(f32), xi)
         + D.astype(f32) * uf)
    y = jax.nn.gelu(y.reshape(Bsz, L, S5_WIDTH))
    y = y * jax.nn.sigmoid(y @ w_glu.astype(f32) + b_glu.astype(f32))
    return y.astype(u.dtype)


def dsa_mixer(hn, w_in, w_out, cos_a, sin_a, cos_i, sin_i):
    f32 = jnp.float32
    Bsz, L, _ = hn.shape
    q, k, v, qi, ki, wi = jnp.split(hn @ w_in, _splits(OD_SIZES), axis=-1)
    q = apply_rope(q.reshape(Bsz, L, ATT_HEADS, ATT_HEAD_DIM), cos_a, sin_a) * ATT_HEAD_DIM ** -0.5
    k = apply_rope(k.reshape(Bsz, L, ATT_KV_HEADS, ATT_HEAD_DIM), cos_a, sin_a)
    v = v.reshape(Bsz, L, ATT_KV_HEADS, ATT_HEAD_DIM)
    qi = apply_rope(qi.reshape(Bsz, L, IDX_HEADS, IDX_DIM), cos_i, sin_i)
    ki = apply_rope(ki.reshape(Bsz, L, 1, IDX_DIM), cos_i, sin_i)[:, :, 0]
    wi = wi * (IDX_HEADS ** -0.5 * IDX_DIM ** -0.5)
    topk = min(TOPK_MAX, L // 4)
    nb = L // Q_BLOCK
    group = ATT_HEADS // ATT_KV_HEADS

    def to_blocks(t):
        return t.reshape(Bsz, nb, Q_BLOCK, *t.shape[2:]).swapaxes(0, 1)

    qpos = jnp.arange(L, dtype=jnp.int32).reshape(nb, Q_BLOCK)
    kpos = jnp.arange(L, dtype=jnp.int32)
    bidx = jnp.arange(Bsz)[:, None, None]

    def block(args):
        qb, qib, wib, tq = args
        logits = jnp.einsum('bqhd,bsd->bqhs', qib, ki)
        score = jnp.einsum('bqhs,bqh->bqs', jax.nn.relu(logits).astype(f32), wib.astype(f32))
        score = jnp.where(kpos[None, None, :] <= tq[None, :, None], score, -jnp.inf)
        _, idx = lax.top_k(score, topk)
        valid = idx <= tq[None, :, None]
        k_sel = k[bidx, idx]
        v_sel = v[bidx, idx]
        qg = qb.reshape(Bsz, Q_BLOCK, ATT_KV_HEADS, group, ATT_HEAD_DIM)
        s = jnp.einsum('bqkgd,bqnkd->bqkgn', qg, k_sel).astype(f32)
        s = jnp.where(valid[:, :, None, None, :], s, -jnp.inf)
        p = jax.nn.softmax(s, axis=-1).astype(v.dtype)
        o = jnp.einsum('bqkgn,bqnkd->bqkgd', p, v_sel)
        return o.reshape(Bsz, Q_BLOCK, ATT_HEADS * ATT_HEAD_DIM)

    out = lax.map(block, (to_blocks(q), to_blocks(qi), to_blocks(wi), qpos))
    out = out.swapaxes(0, 1).reshape(Bsz, L, ATT_HEADS * ATT_HEAD_DIM)
    return out @ w_out


def cross_attn(hn, mem_n, wq, wk, wv, wo):
    Bsz, L, _ = hn.shape
    M = mem_n.shape[1]
    q = (hn @ wq).reshape(Bsz, L, XA_HEADS, XA_HEAD_DIM) * XA_HEAD_DIM ** -0.5
    k = (mem_n @ wk).reshape(Bsz, M, XA_HEADS, XA_HEAD_DIM)
    v = (mem_n @ wv).reshape(Bsz, M, XA_HEADS, XA_HEAD_DIM)
    s = jnp.einsum('bqhd,bmhd->bhqm', q, k).astype(jnp.float32)
    p = jax.nn.softmax(s, axis=-1).astype(hn.dtype)
    o = jnp.einsum('bhqm,bmhd->bqhd', p, v).reshape(Bsz, L, XA_HEADS * XA_HEAD_DIM)
    return o @ wo


def setup_inputs(seed: int = 0) -> dict:
    key = jax.random.key(seed)
    ks = iter(jax.random.split(key, 48))
    f32 = jnp.float32

    def w(shape, fan_in):
        return jax.random.normal(next(ks), shape, f32) * fan_in ** -0.5

    def gain(shape):
        return 1.0 + 0.02 * jax.random.normal(next(ks), shape, f32)

    def small(shape, s=0.01):
        return s * jax.random.normal(next(ks), shape, f32)

    x = jax.random.normal(next(ks), (BATCH, SEQ, D_MODEL), f32)
    mem = jax.random.normal(next(ks), (BATCH, N_MEM, D_MODEL), f32)
    offset = jax.random.randint(next(ks), (BATCH, 1), 0, 1024, dtype=jnp.int32)
    positions = offset + jnp.arange(SEQ, dtype=jnp.int32)[None, :]
    n_idx = jnp.arange(S5_STATE, dtype=f32)
    lam_re = -0.5 + small((N_EVEN, S5_GROUPS, S5_STATE))
    lam_im = math.pi * n_idx + small((N_EVEN, S5_GROUPS, S5_STATE))
    log_dt = jax.random.uniform(next(ks), (N_EVEN, S5_GROUPS), f32,
                                math.log(S5_DT_MIN), math.log(S5_DT_MAX))
    return {
        "x": x,
        "mem": mem,
        "positions": positions,
        "ffn1_norm": gain((DEPTH, D_MODEL)),
        "ffn1_w_gate": w((DEPTH, D_MODEL, D_FF), D_MODEL),
        "ffn1_w_up": w((DEPTH, D_MODEL, D_FF), D_MODEL),
        "ffn1_w_down": w((DEPTH, D_FF, D_MODEL), D_FF),
        "mix_norm": gain((DEPTH, D_MODEL)),
        "ev_w_in": w((N_EVEN, D_MODEL, EV_IN), D_MODEL),
        "ev_w_out": w((N_EVEN, GLA_WIDTH + S5_WIDTH, D_MODEL), GLA_WIDTH + S5_WIDTH),
        "gla_w_alpha": w((N_EVEN, GLA_RANK, GLA_HEADS * GLA_DK), GLA_RANK),
        "gla_b_alpha": small((N_EVEN, GLA_HEADS * GLA_DK), 0.1),
        "gla_norm": gain((N_EVEN, GLA_DV)),
        "s5_lambda_re": lam_re,
        "s5_lambda_im": lam_im,
        "s5_log_dt": log_dt,
        "s5_B_re": w((N_EVEN, S5_GROUPS, S5_STATE, S5_GROUP), 2 * S5_GROUP),
        "s5_B_im": w((N_EVEN, S5_GROUPS, S5_STATE, S5_GROUP), 2 * S5_GROUP),
        "s5_C_re": 0.5 * jax.random.normal(next(ks), (N_EVEN, S5_GROUPS, S5_GROUP, S5_STATE), f32),
        "s5_C_im": 0.5 * jax.random.normal(next(ks), (N_EVEN, S5_GROUPS, S5_GROUP, S5_STATE), f32),
        "s5_D": jax.random.normal(next(ks), (N_EVEN, S5_GROUPS, S5_GROUP), f32),
        "s5_w_glu": w((N_EVEN, S5_WIDTH, S5_WIDTH), S5_WIDTH),
        "s5_b_glu": small((N_EVEN, S5_WIDTH)),
        "od_w_in": w((N_ODD, D_MODEL, OD_IN), D_MODEL),
        "od_w_out": w((N_ODD, ATT_HEADS * ATT_HEAD_DIM, D_MODEL), ATT_HEADS * ATT_HEAD_DIM),
        "xa_norm": gain((DEPTH, D_MODEL)),
        "mem_norm": gain((D_MODEL,)),
        "xa_wq": w((DEPTH, D_MODEL, D_MODEL), D_MODEL),
        "xa_wk": w((DEPTH, D_MODEL, D_MODEL), D_MODEL),
        "xa_wv": w((DEPTH, D_MODEL, D_MODEL), D_MODEL),
        "xa_wo": w((DEPTH, D_MODEL, D_MODEL), D_MODEL),
        "ffn2_norm": gain((DEPTH, D_MODEL)),
        "ffn2_w_gate": w((DEPTH, D_MODEL, D_FF), D_MODEL),
        "ffn2_w_up": w((DEPTH, D_MODEL, D_FF), D_MODEL),
        "ffn2_w_down": w((DEPTH, D_FF, D_MODEL), D_FF),
        "final_norm": gain((D_MODEL,)),
    }


def reference(x, mem, positions, ffn1_norm, ffn1_w_gate, ffn1_w_up, ffn1_w_down, mix_norm,
              ev_w_in, ev_w_out, gla_w_alpha, gla_b_alpha, gla_norm, s5_lambda_re, s5_lambda_im,
              s5_log_dt, s5_B_re, s5_B_im, s5_C_re, s5_C_im, s5_D, s5_w_glu, s5_b_glu,
              od_w_in, od_w_out, xa_norm, mem_norm, xa_wq, xa_wk, xa_wv, xa_wo,
              ffn2_norm, ffn2_w_gate, ffn2_w_up, ffn2_w_down, final_norm):
    cos_a, sin_a = rope_tables(positions, ATT_HEAD_DIM)
    cos_i, sin_i = rope_tables(positions, IDX_DIM)
    mem_n = rmsnorm(mem, mem_norm)
    h = x
    for layer in range(DEPTH):
        h = h + 0.5 * swiglu(rmsnorm(h, ffn1_norm[layer]),
                             ffn1_w_gate[layer], ffn1_w_up[layer], ffn1_w_down[layer])
        hn = rmsnorm(h, mix_norm[layer])
        if layer % 2 == 0:
            e = layer // 2
            q, k, v, r, a_low, u = jnp.split(hn @ ev_w_in[e], _splits(EV_SIZES), axis=-1)
            o_gla = gla_mixer(q, k, v, r, a_low, gla_w_alpha[e], gla_b_alpha[e], gla_norm[e])
            o_s5 = s5_mixer(u, s5_lambda_re[e], s5_lambda_im[e], s5_log_dt[e], s5_B_re[e],
                            s5_B_im[e], s5_C_re[e], s5_C_im[e], s5_D[e], s5_w_glu[e], s5_b_glu[e])
            mix = jnp.concatenate([o_gla, o_s5], axis=-1) @ ev_w_out[e]
        else:
            o = layer // 2
            mix = dsa_mixer(hn, od_w_in[o], od_w_out[o], cos_a, sin_a, cos_i, sin_i)
        h = h + mix
        h = h + cross_attn(rmsnorm(h, xa_norm[layer]), mem_n,
                           xa_wq[layer], xa_wk[layer], xa_wv[layer], xa_wo[layer])
        h = h + 0.5 * swiglu(rmsnorm(h, ffn2_norm[layer]),
                             ffn2_w_gate[layer], ffn2_w_up[layer], ffn2_w_down[layer])
    return rmsnorm(h, final_norm)
```

```python
import functools
import math

import jax
import jax.numpy as jnp
from jax import lax
from jax.experimental import pallas as pl
from jax.experimental.pallas import tpu as pltpu

F32 = jnp.float32
BF16 = jnp.bfloat16
I32 = jnp.int32

EPS = 1e-6
ROPE_THETA = 10000.0
GLA_HEADS, GLA_DK, GLA_DV, GLA_RANK, GLA_TAU, GLA_CHUNK = 4, 64, 128, 16, 16.0, 64
S5_GROUP, S5_STATE = 16, 64
ATT_HEADS, ATT_KV_HEADS, ATT_HEAD_DIM = 8, 2, 128
IDX_HEADS, IDX_DIM = 8, 64
TOPK_MAX = 256
XA_HEADS = 4

LANES = 128
VMEM_LIMIT = 56 * 1024 * 1024
ROW_TILE = 512
FFN_CHUNK = 256
GLA_TILE = 256
S5_TILE = 256
S5_SPLIT = 4
DSA_TQ = 256
DSA_TK = 512
NEG = -1e30

_NT = (((1,), (1,)), ((), ()))
_TN = (((0,), (0,)), ((), ()))


def _params(*sem):
    return pltpu.CompilerParams(dimension_semantics=sem, vmem_limit_bytes=VMEM_LIMIT)


def _const_spec(a):
    nd = a.ndim
    return pl.BlockSpec(a.shape, lambda *_: (0,) * nd, pipeline_mode=pl.Buffered(1))


def _rms(x, g):
    return x * lax.rsqrt(jnp.mean(x * x, axis=-1, keepdims=True) + EPS) * g


def _dot(a, b):
    return jnp.dot(a, b, preferred_element_type=F32)


def _row_tiled(body, rows, consts, outs, tm, name):
    m = rows[0].shape[0]
    assert m % tm == 0
    return pl.pallas_call(
        body,
        grid=(m // tm,),
        in_specs=[pl.BlockSpec((tm, r.shape[1]), lambda i: (i, 0)) for r in rows]
        + [_const_spec(c) for c in consts],
        out_specs=[pl.BlockSpec((tm, w), lambda i: (i, 0)) for w, _ in outs],
        out_shape=[jax.ShapeDtypeStruct((m, w), dt) for w, dt in outs],
        compiler_params=_params("parallel"),
        name=name,
    )(*rows, *consts)


def _ffn_body(h_ref, g_ref, wg_ref, wu_ref, wd_ref, *rest, final):
    o_ref = rest[-1]
    x = h_ref[...]
    n = _rms(x, g_ref[...]).astype(BF16)
    acc = jnp.zeros_like(x)
    for c in range(wg_ref.shape[1] // FFN_CHUNK):
        sl = slice(c * FFN_CHUNK, (c + 1) * FFN_CHUNK)
        g = _dot(n, wg_ref[:, sl])
        u = _dot(n, wu_ref[:, sl])
        a = (g * jax.nn.sigmoid(g) * u).astype(BF16)
        acc = acc + _dot(a, wd_ref[sl, :])
    y = x + 0.5 * acc
    if final:
        y = _rms(y, rest[0][...])
    o_ref[...] = y


def _ffn(h, g, wg, wu, wd, final_g=None):
    d = h.shape[1]
    consts = [g.reshape(1, d), wg.astype(BF16), wu.astype(BF16), wd.astype(BF16)]
    if final_g is not None:
        consts.append(final_g.reshape(1, d))
    body = functools.partial(_ffn_body, final=final_g is not None)
    return _row_tiled(body, [h], consts, [(d, F32)], ROW_TILE, "ffn")[0]


def _proj_res_body(*refs, n_x):
    h_ref, o_ref = refs[0], refs[-1]
    acc = h_ref[...]
    for x_ref, w_ref in zip(refs[1:1 + n_x], refs[1 + n_x:1 + 2 * n_x]):
        acc = acc + _dot(x_ref[...].astype(BF16), w_ref[...])
    o_ref[...] = acc


def _proj_res(h, xs, ws):
    body = functools.partial(_proj_res_body, n_x=len(xs))
    return _row_tiled(body, [h] + list(xs), [w.astype(BF16) for w in ws],
                      [(h.shape[1], F32)], ROW_TILE, "proj_res")[0]


def _norm_proj_body(x_ref, g_ref, w_ref, o_ref):
    n = _rms(x_ref[...], g_ref[...]).astype(BF16)
    o_ref[...] = _dot(n, w_ref[...]).astype(o_ref.dtype)


def _norm_proj(x, g, w, tm, out_dtype):
    return _row_tiled(_norm_proj_body, [x], [g.reshape(1, -1), w.astype(BF16)],
                      [(w.shape[1], out_dtype)], tm, "norm_proj")[0]


def _log_sigmoid(x):
    return jnp.minimum(x, 0.0) - jnp.log(1.0 + jnp.exp(-jnp.abs(x)))


def _ev_in_body(h_ref, g_ref, wm_ref, wa_ref, wa2_ref, ba_ref, wu_ref,
                qk_ref, v_ref, r_ref, la_ref, u_ref):
    n = _rms(h_ref[...], g_ref[...]).astype(BF16)
    nqk = qk_ref.shape[1]
    nv = v_ref.shape[1]
    main = _dot(n, wm_ref[...])
    qk_ref[...] = main[:, :nqk]
    v_ref[...] = main[:, nqk:nqk + nv]
    r_ref[...] = main[:, nqk + nv:]
    a_low = _dot(n, wa_ref[...]).astype(BF16)
    alpha = _dot(a_low, wa2_ref[...]) + ba_ref[...]
    la_ref[...] = _log_sigmoid(alpha) / GLA_TAU
    u_ref[...] = _dot(n, wu_ref[...])


def _ev_in(h, g, w_in, w_a2, b_a):
    d = h.shape[1]
    nqk = 2 * GLA_HEADS * GLA_DK
    nv = GLA_HEADS * GLA_DV
    n_main = nqk + 2 * nv
    w_main = w_in[:, :n_main]
    w_a = jnp.pad(w_in[:, n_main:n_main + GLA_RANK], ((0, 0), (0, LANES - GLA_RANK)))
    w_u = w_in[:, n_main + GLA_RANK:]
    w_a2p = jnp.pad(w_a2, ((0, LANES - GLA_RANK), (0, 0)))
    consts = [g.reshape(1, d), w_main.astype(BF16), w_a.astype(BF16), w_a2p.astype(BF16),
              b_a.reshape(1, -1), w_u.astype(BF16)]
    outs = [(nqk, F32), (nv, F32), (nv, F32), (GLA_HEADS * GLA_DK, F32), (w_u.shape[1], F32)]
    return _row_tiled(_ev_in_body, [h], consts, outs, ROW_TILE, "ev_in")


def _gla_body(qk_ref, v_ref, r_ref, la_ref, gn_ref, o_ref, st_ref):
    tl = qk_ref.shape[0]
    hdk = GLA_HEADS * GLA_DK
    c = GLA_CHUNK

    @pl.when(pl.program_id(1) == 0)
    def _():
        st_ref[...] = jnp.zeros_like(st_ref)

    row = lax.broadcasted_iota(I32, (tl, hdk), 0) % c
    bc = la_ref[...]
    d = 1
    while d < c:
        bc = bc + jnp.where(row >= d, pltpu.roll(bc, d, axis=0), 0.0)
        d *= 2
    q_dec = qk_ref[:, :hdk] * (GLA_DK ** -0.5) * jnp.exp(bc)
    k = qk_ref[:, hdk:]
    k_inv = k * jnp.exp(-bc)
    causal = lax.broadcasted_iota(I32, (c, c), 0) >= lax.broadcasted_iota(I32, (c, c), 1)
    gn = gn_ref[...]
    for j in range(tl // c):
        rs = slice(j * c, (j + 1) * c)
        b_last = bc[j * c + c - 1:j * c + c, :]
        k_end = k[rs] * jnp.exp(b_last - bc[rs])
        dec = jnp.exp(b_last)
        for h in range(GLA_HEADS):
            ks = slice(h * GLA_DK, (h + 1) * GLA_DK)
            vs = slice(h * GLA_DV, (h + 1) * GLA_DV)
            qd = q_dec[rs, ks].astype(BF16)
            vh = v_ref[rs, vs].astype(BF16)
            s = lax.dot_general(qd, k_inv[rs, ks].astype(BF16), _NT, preferred_element_type=F32)
            s = jnp.where(causal, s, 0.0).astype(BF16)
            st = st_ref[h]
            o = _dot(s, vh) + lax.dot_general(qd, st.astype(BF16), _NT, preferred_element_type=F32)
            st_ref[h] = st * dec[:, ks] + lax.dot_general(
                vh, k_end[:, ks].astype(BF16), _TN, preferred_element_type=F32)
            o = o * lax.rsqrt(jnp.mean(o * o, axis=-1, keepdims=True) + EPS) * gn
            r = r_ref[rs, vs]
            o_ref[rs, vs] = o * (r * jax.nn.sigmoid(r))


def _gla(qk, v, r, la, gn):
    b, l, _ = qk.shape
    tl = GLA_TILE

    def spec(w):
        return pl.BlockSpec((None, tl, w), lambda bi, t: (bi, t, 0))

    return pl.pallas_call(
        _gla_body,
        grid=(b, l // tl),
        in_specs=[spec(qk.shape[2]), spec(v.shape[2]), spec(r.shape[2]), spec(la.shape[2]),
                  pl.BlockSpec((1, GLA_DV), lambda bi, t: (0, 0))],
        out_specs=spec(v.shape[2]),
        out_shape=jax.ShapeDtypeStruct(v.shape, F32),
        scratch_shapes=[pltpu.VMEM((GLA_HEADS, GLA_DV, GLA_DK), F32)],
        compiler_params=_params("parallel", "arbitrary"),
        name="gla",
    )(qk, v, r, la, gn.reshape(1, GLA_DV))


def _cmul(ar, ai, br, bi):
    return ar * br - ai * bi, ar * bi + ai * br


def _s5_prep_body(lr_ref, li_ref, ldt_ref, br_ref, bi_ref, bbr_ref, bbi_ref, pr_ref, pi_ref):
    lr = jnp.minimum(lr_ref[...], -1e-4)
    li = li_ref[...]
    dt = jnp.exp(ldt_ref[...])
    mag = jnp.exp(lr * dt)
    ar = mag * jnp.cos(li * dt)
    ai = mag * jnp.sin(li * dt)
    den = lr * lr + li * li
    nr = ar - 1.0
    fr = (nr * lr + ai * li) / den
    fi = (ai * lr - nr * li) / den
    w = br_ref.shape[2]
    for j in range(br_ref.shape[0]):
        sl = slice(j * w, (j + 1) * w)
        b_re, b_im = br_ref[j], bi_ref[j]
        bbr_ref[j] = (fr[:, sl] * b_re - fi[:, sl] * b_im).astype(BF16)
        bbi_ref[j] = (fr[:, sl] * b_im + fi[:, sl] * b_re).astype(BF16)
    pr_ref[0:1, :] = ar
    pi_ref[0:1, :] = ai
    n = 1
    while n < pr_ref.shape[0]:
        sr, si = pr_ref[n - 1:n, :], pi_ref[n - 1:n, :]
        nr_, ni_ = _cmul(pr_ref[0:n, :], pi_ref[0:n, :], sr, si)
        pr_ref[n:2 * n, :] = nr_
        pi_ref[n:2 * n, :] = ni_
        n *= 2


def _s5_body(u_ref, bbr_ref, bbi_ref, pr_ref, pi_ref, cr_ref, ci_ref, d_ref, wg_ref, bg_ref,
             o_ref, sr_ref, si_ref):
    t = u_ref.shape[0]
    w = bbr_ref.shape[2]
    cw = bbr_ref.shape[1]

    @pl.when(pl.program_id(1) == 0)
    def _():
        sr_ref[...] = jnp.zeros_like(sr_ref)
        si_ref[...] = jnp.zeros_like(si_ref)

    u = u_ref[...]
    ub = u.astype(BF16)
    row = lax.broadcasted_iota(I32, (t, w), 0)
    ys = []
    for j in range(bbr_ref.shape[0]):
        ls = slice(j * w, (j + 1) * w)
        uj = ub[:, j * cw:(j + 1) * cw]
        xr = _dot(uj, bbr_ref[j])
        xi = _dot(uj, bbi_ref[j])
        d = 1
        while d < t:
            ar, ai = pr_ref[d - 1:d, ls], pi_ref[d - 1:d, ls]
            if d < 8:
                shr = jnp.where(row >= d, pltpu.roll(xr, d, axis=0), 0.0)
                shi = jnp.where(row >= d, pltpu.roll(xi, d, axis=0), 0.0)
            else:
                z = jnp.zeros((d, w), F32)
                shr = jnp.concatenate([z, xr[:t - d]], axis=0)
                shi = jnp.concatenate([z, xi[:t - d]], axis=0)
            dr, di = _cmul(ar, ai, shr, shi)
            xr, xi = xr + dr, xi + di
            d *= 2
        dr, di = _cmul(pr_ref[:, ls], pi_ref[:, ls], sr_ref[:, ls], si_ref[:, ls])
        xr, xi = xr + dr, xi + di
        sr_ref[:, ls] = xr[t - 1:t]
        si_ref[:, ls] = xi[t - 1:t]
        ys.append(_dot(xr.astype(BF16), cr_ref[j]) - _dot(xi.astype(BF16), ci_ref[j]))
    y = jnp.concatenate(ys, axis=1) + d_ref[...] * u
    y = jax.nn.gelu(y)
    gate = jax.nn.sigmoid(_dot(y.astype(BF16), wg_ref[...]) + bg_ref[...])
    o_ref[...] = y * gate


def _block_diag(m, split):
    g, r, c = m.shape
    gs = g // split
    eye = jnp.eye(gs, dtype=m.dtype)
    m = m.reshape(split, gs, r, c)
    out = m[:, :, :, None, :] * eye[None, :, None, :, None]
    return out.reshape(split, gs * r, gs * c)


def _s5(u, lam_re, lam_im, log_dt, b_re, b_im, c_re, c_im, dd, w_glu, b_glu):
    b, l, width = u.shape
    g, p = lam_re.shape
    n_state = g * p
    t = S5_TILE
    br_bd = _block_diag(jnp.swapaxes(b_re, 1, 2), S5_SPLIT)
    bi_bd = _block_diag(jnp.swapaxes(b_im, 1, 2), S5_SPLIT)
    cr_bd = _block_diag(jnp.swapaxes(c_re, 1, 2), S5_SPLIT).astype(BF16)
    ci_bd = _block_diag(jnp.swapaxes(c_im, 1, 2), S5_SPLIT).astype(BF16)
    ldt = jnp.broadcast_to(log_dt[:, None], (g, p)).reshape(1, n_state)
    bbr, bbi, pw_r, pw_i = pl.pallas_call(
        _s5_prep_body,
        out_shape=[jax.ShapeDtypeStruct(br_bd.shape, BF16), jax.ShapeDtypeStruct(br_bd.shape, BF16),
                   jax.ShapeDtypeStruct((t, n_state), F32), jax.ShapeDtypeStruct((t, n_state), F32)],
        compiler_params=pltpu.CompilerParams(vmem_limit_bytes=VMEM_LIMIT),
        name="s5_prep",
    )(lam_re.reshape(1, n_state), lam_im.reshape(1, n_state), ldt, br_bd, bi_bd)

    consts = [bbr, bbi, pw_r, pw_i, cr_bd, ci_bd, dd.reshape(1, width), w_glu.astype(BF16),
              b_glu.reshape(1, width)]
    return pl.pallas_call(
        _s5_body,
        grid=(b, l // t),
        in_specs=[pl.BlockSpec((None, t, width), lambda bi_, ti: (bi_, ti, 0))]
        + [_const_spec(c) for c in consts],
        out_specs=pl.BlockSpec((None, t, width), lambda bi_, ti: (bi_, ti, 0)),
        out_shape=jax.ShapeDtypeStruct(u.shape, F32),
        scratch_shapes=[pltpu.VMEM((1, n_state), F32), pltpu.VMEM((1, n_state), F32)],
        compiler_params=_params("parallel", "arbitrary"),
        name="s5",
    )(u, *consts)


def _rope(x, cos, sin_signed, half):
    lane = lax.broadcasted_iota(I32, x.shape, 1) % (2 * half)
    rot = jnp.where(lane < half, pltpu.roll(x, LANES - half, axis=1), pltpu.roll(x, half, axis=1))
    return x * cos + rot * sin_signed


def _od_in_body(h_ref, ca_ref, sa_ref, ci_ref, si_ref, g_ref, wm_ref, wk_ref, ww_ref,
                q_ref, k_ref, v_ref, qi_ref, ki_ref, wi_ref):
    n = _rms(h_ref[...], g_ref[...]).astype(BF16)
    main = _dot(n, wm_ref[...])
    ca, sa, ci, si = ca_ref[...], sa_ref[...], ci_ref[...], si_ref[...]
    hd = ATT_HEAD_DIM
    nq, nk = q_ref.shape[1], k_ref.shape[1]
    for h in range(nq // hd):
        sl = slice(h * hd, (h + 1) * hd)
        q_ref[:, sl] = (_rope(main[:, sl], ca, sa, hd // 2) * (hd ** -0.5)).astype(BF16)
    for h in range(nk // hd):
        k_ref[:, h * hd:(h + 1) * hd] = _rope(
            main[:, nq + h * hd:nq + (h + 1) * hd], ca, sa, hd // 2).astype(BF16)
    v_ref[...] = main[:, nq + nk:nq + 2 * nk].astype(BF16)
    o = nq + 2 * nk
    for s in range(qi_ref.shape[1] // LANES):
        sl = slice(s * LANES, (s + 1) * LANES)
        qi_ref[:, sl] = _rope(main[:, o + s * LANES:o + (s + 1) * LANES], ci, si,
                              IDX_DIM // 2).astype(BF16)
    ki_ref[...] = _rope(_dot(n, wk_ref[...]), ci, si, IDX_DIM // 2).astype(BF16)
    wi_ref[...] = _dot(n, ww_ref[...]) * (IDX_HEADS ** -0.5 * IDX_DIM ** -0.5)


def _rope_tables(positions, dim):
    inv = ROPE_THETA ** (-jnp.arange(0, dim, 2, dtype=F32) / dim)
    ang = positions.astype(F32)[..., None] * inv
    cos = jnp.cos(ang)
    sin = jnp.sin(ang)
    reps = LANES // dim
    cos = jnp.tile(jnp.concatenate([cos, cos], -1), (1, 1, reps))
    sin = jnp.tile(jnp.concatenate([-sin, sin], -1), (1, 1, reps))
    return cos.reshape(-1, LANES), sin.reshape(-1, LANES)


def _od_in(h, g, w_in, tables):
    d = h.shape[1]
    nq = ATT_HEADS * ATT_HEAD_DIM
    nk = ATT_KV_HEADS * ATT_HEAD_DIM
    nqi = IDX_HEADS * IDX_DIM
    n_main = nq + 2 * nk + nqi
    w_main = w_in[:, :n_main]
    w_ki = jnp.pad(w_in[:, n_main:n_main + IDX_DIM], ((0, 0), (0, LANES - IDX_DIM)))
    w_wi = jnp.pad(w_in[:, n_main + IDX_DIM:], ((0, 0), (0, LANES - IDX_HEADS)))
    consts = [g.reshape(1, d), w_main.astype(BF16), w_ki.astype(BF16), w_wi.astype(BF16)]
    outs = [(nq, BF16), (nk, BF16), (nk, BF16), (nqi, BF16), (LANES, BF16), (LANES, F32)]
    return _row_tiled(_od_in_body, [h] + list(tables), consts, outs, ROW_TILE, "od_in")


def _sort_key(x):
    bits = lax.bitcast_convert_type(x, I32)
    return bits ^ ((bits >> 31) & jnp.int32(0x7FFFFFFF))


def _dsa_body(qi_ref, wi_ref, q_ref, ki_ref, k_ref, v_ref, o_ref,
              sc_ref, m_ref, l_ref, acc_ref, *, topk):
    tq = q_ref.shape[0]
    tk = sc_ref.shape[2]
    hd = ATT_HEAD_DIM
    group = ATT_HEADS // ATT_KV_HEADS
    r0 = pl.program_id(1) * tq
    nkb = (r0 + tq + tk - 1) // tk
    qpos = r0 + lax.broadcasted_iota(I32, (tq, 1), 0)
    key_ninf = jnp.int32(-2139095041)

    wi = wi_ref[...]

    def score_block(kb, carry):
        c0 = pl.multiple_of(kb * tk, tk)
        kib = ki_ref[pl.ds(c0, tk), :][:, :IDX_DIM]
        acc = jnp.zeros((tq, tk), F32)
        for h in range(IDX_HEADS):
            lg = lax.dot_general(qi_ref[:, h * IDX_DIM:(h + 1) * IDX_DIM], kib, _NT,
                                 preferred_element_type=F32)
            acc = acc + jnp.maximum(lg, 0.0) * wi[:, h:h + 1]
        acc = jnp.where(acc == 0.0, 0.0, acc)
        kpos = c0 + lax.broadcasted_iota(I32, (1, tk), 1)
        sc_ref[kb] = jnp.where(kpos <= qpos, _sort_key(acc), key_ninf)
        return carry

    lax.fori_loop(0, nkb, score_block, 0)

    def count(pred):
        def body(kb, cnt):
            ind = jnp.where(pred(sc_ref[kb], kb), 1.0, 0.0)
            for s in range(tk // LANES):
                cnt = cnt + ind[:, s * LANES:(s + 1) * LANES]
            return cnt
        cnt = lax.fori_loop(0, nkb, body, jnp.zeros((tq, LANES), F32))
        return jnp.sum(cnt, axis=1, keepdims=True)

    kf = jnp.float32(topk)
    short = (qpos + 1) <= topk
    int_min = jnp.int32(-2147483648)

    def bis_cond(st):
        b, _, _, pending = st
        return jnp.logical_and(b >= 0, pending > 0.0)

    def bis_body(st):
        b, ans, cnt_ans, _ = st
        cand = ans + lax.shift_left(jnp.int32(1), b)
        cnt = count(lambda keys, kb: keys >= cand)
        take = cnt >= kf
        ans = jnp.where(take, cand, ans)
        cnt_ans = jnp.where(take, cnt, cnt_ans)
        settled = jnp.logical_or(cnt_ans == kf, short)
        pending = jnp.sum(jnp.where(settled, 0.0, 1.0))
        return b - 1, ans, cnt_ans, pending

    total = (nkb * tk).astype(F32)
    st0 = (jnp.int32(31), jnp.full((tq, 1), int_min, I32), jnp.full((tq, 1), total, F32),
           jnp.float32(1.0))
    _, thr, cnt_thr, _ = lax.while_loop(bis_cond, bis_body, st0)
    thr = jnp.maximum(thr, key_ninf + 1)

    tied = jnp.logical_and(cnt_thr > kf, jnp.logical_not(short))
    n_tied = jnp.sum(jnp.where(tied, 1.0, 0.0))

    @pl.when(n_tied > 0.0)
    def _():
        n_gt = count(lambda keys, kb: keys > thr)
        need = kf - n_gt

        def kpos_of(kb):
            return kb * tk + lax.broadcasted_iota(I32, (1, tk), 1)

        def idx_body(i, lim):
            cand = lim + lax.shift_left(jnp.int32(1), 30 - i)
            cnt = count(lambda keys, kb: jnp.logical_and(keys == thr, kpos_of(kb) < cand))
            return jnp.where(cnt < need, cand, lim)

        lim = lax.fori_loop(0, 31, idx_body, jnp.zeros((tq, 1), I32))

        def demote(kb, carry):
            keys = sc_ref[kb]
            drop = jnp.logical_and(jnp.logical_and(keys == thr, kpos_of(kb) > lim), tied)
            sc_ref[kb] = jnp.where(drop, key_ninf, keys)
            return carry

        lax.fori_loop(0, nkb, demote, 0)

    def to_bias(kb, carry):
        bias = jnp.where(sc_ref[kb] >= thr, 0.0, NEG).astype(F32)
        sc_ref[kb] = lax.bitcast_convert_type(bias, I32)
        return carry

    lax.fori_loop(0, nkb, to_bias, 0)

    m_ref[...] = jnp.full_like(m_ref, NEG)
    l_ref[...] = jnp.zeros_like(l_ref)
    acc_ref[...] = jnp.zeros_like(acc_ref)

    def attend(kb, carry):
        c0 = pl.multiple_of(kb * tk, tk)
        bias = lax.bitcast_convert_type(sc_ref[kb], F32)
        for h in range(ATT_HEADS):
            kv = h // group
            hs = slice(h * hd, (h + 1) * hd)
            kblk = k_ref[pl.ds(c0, tk), kv * hd:(kv + 1) * hd]
            vblk = v_ref[pl.ds(c0, tk), kv * hd:(kv + 1) * hd]
            s = lax.dot_general(q_ref[:, hs], kblk, _NT, preferred_element_type=F32) + bias
            m_old = m_ref[h]
            m_new = jnp.maximum(m_old, jnp.max(s, axis=1, keepdims=True))
            alpha = jnp.exp(m_old - m_new)
            p = jnp.exp(s - m_new)
            l_ref[h] = alpha * l_ref[h] + jnp.sum(p, axis=1, keepdims=True)
            acc_ref[:, hs] = alpha * acc_ref[:, hs] + _dot(p.astype(BF16), vblk)
            m_ref[h] = m_new
        return carry

    lax.fori_loop(0, nkb, attend, 0)
    for h in range(ATT_HEADS):
        hs = slice(h * hd, (h + 1) * hd)
        o_ref[:, hs] = (acc_ref[:, hs] / l_ref[h]).astype(o_ref.dtype)


def _dsa(q, k, v, qi, ki, wi):
    b, l, _ = q.shape
    tq, tk = min(DSA_TQ, l), min(DSA_TK, l)
    topk = min(TOPK_MAX, l // 4)

    def qspec(w):
        return pl.BlockSpec((None, tq, w), lambda bi, i: (bi, i, 0))

    def kspec(w):
        return pl.BlockSpec((None, l, w), lambda bi, i: (bi, 0, 0))

    return pl.pallas_call(
        functools.partial(_dsa_body, topk=topk),
        grid=(b, l // tq),
        in_specs=[qspec(qi.shape[2]), qspec(wi.shape[2]), qspec(q.shape[2]),
                  kspec(ki.shape[2]), kspec(k.shape[2]), kspec(v.shape[2])],
        out_specs=qspec(q.shape[2]),
        out_shape=jax.ShapeDtypeStruct(q.shape, BF16),
        scratch_shapes=[pltpu.VMEM((l // tk, tq, tk), I32),
                        pltpu.VMEM((ATT_HEADS, tq, 1), F32),
                        pltpu.VMEM((ATT_HEADS, tq, 1), F32),
                        pltpu.VMEM((tq, q.shape[2]), F32)],
        compiler_params=_params("parallel", "arbitrary"),
        name="dsa",
    )(qi, wi, q, ki, k, v)


def _xattn_body(h_ref, g_ref, wq_ref, kv_ref, wo_ref, o_ref):
    x = h_ref[...]
    d = x.shape[1]
    hd = d // XA_HEADS
    n = _rms(x, g_ref[...]).astype(BF16)
    q = (_dot(n, wq_ref[...]) * (hd ** -0.5)).astype(BF16)
    outs = []
    for h in range(XA_HEADS):
        hs = slice(h * hd, (h + 1) * hd)
        s = lax.dot_general(q[:, hs], kv_ref[:, hs], _NT, preferred_element_type=F32)
        p = jnp.exp(s - jnp.max(s, axis=1, keepdims=True))
        p = p / jnp.sum(p, axis=1, keepdims=True)
        outs.append(_dot(p.astype(BF16), kv_ref[:, d + h * hd:d + (h + 1) * hd]))
    o = jnp.concatenate(outs, axis=1).astype(BF16)
    o_ref[...] = x + _dot(o, wo_ref[...])


def _xattn(h, g, wq, kv, wo, rows_per_batch):
    m, d = h.shape
    tm = ROW_TILE
    per = rows_per_batch // tm
    consts = [g.reshape(1, d), wq.astype(BF16)]
    wo_b = wo.astype(BF16)
    return pl.pallas_call(
        _xattn_body,
        grid=(m // tm,),
        in_specs=[pl.BlockSpec((tm, d), lambda i: (i, 0))] + [_const_spec(c) for c in consts]
        + [pl.BlockSpec((None,) + kv.shape[1:], lambda i: (i // per, 0, 0)), _const_spec(wo_b)],
        out_specs=pl.BlockSpec((tm, d), lambda i: (i, 0)),
        out_shape=jax.ShapeDtypeStruct((m, d), F32),
        compiler_params=_params("parallel"),
        name="xattn",
    )(h, *consts, kv, wo_b)


def kernel(x, mem, positions, ffn1_norm, ffn1_w_gate, ffn1_w_up, ffn1_w_down, mix_norm, ev_w_in, ev_w_out, gla_w_alpha, gla_b_alpha, gla_norm, s5_lambda_re, s5_lambda_im, s5_log_dt, s5_B_re, s5_B_im, s5_C_re, s5_C_im, s5_D, s5_w_glu, s5_b_glu, od_w_in, od_w_out, xa_norm, mem_norm, xa_wq, xa_wk, xa_wv, xa_wo, ffn2_norm, ffn2_w_gate, ffn2_w_up, ffn2_w_down, final_norm):
    b, l, d = x.shape
    n_mem = mem.shape[1]
    depth = ffn1_norm.shape[0]
    tables = _rope_tables(positions, ATT_HEAD_DIM) + _rope_tables(positions, IDX_DIM)
    mem2 = mem.reshape(b * n_mem, d)
    h = x.reshape(b * l, d)

    def seq(a):
        return a.reshape(b, l, a.shape[-1])

    def flat(a):
        return a.reshape(b * l, a.shape[-1])

    for layer in range(depth):
        h = _ffn(h, ffn1_norm[layer], ffn1_w_gate[layer], ffn1_w_up[layer], ffn1_w_down[layer])
        if layer % 2 == 0:
            e = layer // 2
            qk, v, r, la, u = _ev_in(h, mix_norm[layer], ev_w_in[e], gla_w_alpha[e], gla_b_alpha[e])
            o_gla = _gla(seq(qk), seq(v), seq(r), seq(la), gla_norm[e])
            o_s5 = _s5(seq(u), s5_lambda_re[e], s5_lambda_im[e], s5_log_dt[e], s5_B_re[e],
                       s5_B_im[e], s5_C_re[e], s5_C_im[e], s5_D[e], s5_w_glu[e], s5_b_glu[e])
            gw = o_gla.shape[-1]
            h = _proj_res(h, [flat(o_gla), flat(o_s5)], [ev_w_out[e][:gw], ev_w_out[e][gw:]])
        else:
            o = layer // 2
            q, k, v, qi, ki, wi = _od_in(h, mix_norm[layer], od_w_in[o], tables)
            att = _dsa(seq(q), seq(k), seq(v), seq(qi), seq(ki), seq(wi))
            h = _proj_res(h, [flat(att)], [od_w_out[o]])
        kv = _norm_proj(mem2, mem_norm, jnp.concatenate([xa_wk[layer], xa_wv[layer]], axis=1),
                        n_mem, BF16)
        h = _xattn(h, xa_norm[layer], xa_wq[layer], kv.reshape(b, n_mem, 2 * d), xa_wo[layer], l)
        last = layer == depth - 1
        h = _ffn(h, ffn2_norm[layer], ffn2_w_gate[layer], ffn2_w_up[layer], ffn2_w_down[layer],
                 final_g=final_norm if last else None)
    return h.reshape(b, l, d)
```

```python
import functools
import math

import jax
import jax.numpy as jnp
from jax import lax
from jax.experimental import pallas as pl
from jax.experimental.pallas import tpu as pltpu

F32 = jnp.float32
BF16 = jnp.bfloat16
I32 = jnp.int32

EPS = 1e-6
ROPE_THETA = 10000.0
GLA_HEADS, GLA_DK, GLA_DV, GLA_RANK, GLA_TAU, GLA_CHUNK = 4, 64, 128, 16, 16.0, 64
S5_GROUP, S5_STATE = 16, 64
ATT_HEADS, ATT_KV_HEADS, ATT_HEAD_DIM = 8, 2, 128
IDX_HEADS, IDX_DIM = 8, 64
TOPK_MAX = 256
XA_HEADS = 4

LANES = 128
VMEM_LIMIT = 56 * 1024 * 1024
ROW_TILE = 512
FFN_CHUNK = 256
GLA_TILE = 256
S5_TILE = 256
S5_SPLIT = 4
DSA_TQ = 256
DSA_TK = 512
NEG = -1e30

_NT = (((1,), (1,)), ((), ()))
_TN = (((0,), (0,)), ((), ()))


def _params(*sem):
    return pltpu.CompilerParams(dimension_semantics=sem, vmem_limit_bytes=VMEM_LIMIT)


def _const_spec(a):
    nd = a.ndim
    return pl.BlockSpec(a.shape, lambda *_: (0,) * nd, pipeline_mode=pl.Buffered(1))


def _rms(x, g):
    return x * lax.rsqrt(jnp.mean(x * x, axis=-1, keepdims=True) + EPS) * g


def _dot(a, b):
    return jnp.dot(a, b, preferred_element_type=F32)


def _row_tiled(body, rows, consts, outs, tm, name):
    m = rows[0].shape[0]
    assert m % tm == 0

    def out_desc(o):
        if len(o) == 2:
            return pl.BlockSpec((tm, o[0]), lambda i: (i, 0)), jax.ShapeDtypeStruct((m, o[0]), o[1])
        return (pl.BlockSpec((o[0], tm, o[1]), lambda i: (0, i, 0)),
                jax.ShapeDtypeStruct((o[0], m, o[1]), o[2]))

    descs = [out_desc(o) for o in outs]
    return pl.pallas_call(
        body,
        grid=(m // tm,),
        in_specs=[pl.BlockSpec((tm, r.shape[1]), lambda i: (i, 0)) for r in rows]
        + [_const_spec(c) for c in consts],
        out_specs=[d[0] for d in descs],
        out_shape=[d[1] for d in descs],
        compiler_params=_params("parallel"),
        name=name,
    )(*rows, *consts)


def _ffn_body(h_ref, g_ref, wg_ref, wu_ref, wd_ref, *rest, final):
    o_ref = rest[-1]
    x = h_ref[...]
    n = _rms(x, g_ref[...]).astype(BF16)
    acc = jnp.zeros_like(x)
    for c in range(wg_ref.shape[1] // FFN_CHUNK):
        sl = slice(c * FFN_CHUNK, (c + 1) * FFN_CHUNK)
        g = _dot(n, wg_ref[:, sl])
        u = _dot(n, wu_ref[:, sl])
        a = (g * jax.nn.sigmoid(g) * u).astype(BF16)
        acc = acc + _dot(a, wd_ref[sl, :])
    y = x + 0.5 * acc
    if final:
        y = _rms(y, rest[0][...])
    o_ref[...] = y


def _ffn(h, g, wg, wu, wd, final_g=None):
    d = h.shape[1]
    consts = [g.reshape(1, d), wg.astype(BF16), wu.astype(BF16), wd.astype(BF16)]
    if final_g is not None:
        consts.append(final_g.reshape(1, d))
    body = functools.partial(_ffn_body, final=final_g is not None)
    return _row_tiled(body, [h], consts, [(d, F32)], ROW_TILE, "ffn")[0]


def _proj_res_body(*refs, n_x):
    h_ref, o_ref = refs[0], refs[-1]
    acc = h_ref[...]
    for x_ref, w_ref in zip(refs[1:1 + n_x], refs[1 + n_x:1 + 2 * n_x]):
        acc = acc + _dot(x_ref[...].astype(BF16), w_ref[...])
    o_ref[...] = acc


def _proj_res(h, xs, ws):
    body = functools.partial(_proj_res_body, n_x=len(xs))
    return _row_tiled(body, [h] + list(xs), [w.astype(BF16) for w in ws],
                      [(h.shape[1], F32)], ROW_TILE, "proj_res")[0]


def _norm_proj_body(x_ref, g_ref, w_ref, o_ref):
    n = _rms(x_ref[...], g_ref[...]).astype(BF16)
    o_ref[...] = _dot(n, w_ref[...]).astype(o_ref.dtype)


def _norm_proj(x, g, w, tm, out_dtype):
    return _row_tiled(_norm_proj_body, [x], [g.reshape(1, -1), w.astype(BF16)],
                      [(w.shape[1], out_dtype)], tm, "norm_proj")[0]


def _log_sigmoid(x):
    return jnp.minimum(x, 0.0) - jnp.log(1.0 + jnp.exp(-jnp.abs(x)))


def _ev_in_body(h_ref, g_ref, wm_ref, wa_ref, wa2_ref, ba_ref, wu_ref,
                qk_ref, v_ref, r_ref, la_ref, u_ref):
    n = _rms(h_ref[...], g_ref[...]).astype(BF16)
    nqk = qk_ref.shape[1]
    nv = v_ref.shape[1]
    main = _dot(n, wm_ref[...])
    qk_ref[...] = main[:, :nqk]
    v_ref[...] = main[:, nqk:nqk + nv]
    r_ref[...] = main[:, nqk + nv:]
    a_low = _dot(n, wa_ref[...]).astype(BF16)
    alpha = _dot(a_low, wa2_ref[...]) + ba_ref[...]
    la_ref[...] = _log_sigmoid(alpha) / GLA_TAU
    u_ref[...] = _dot(n, wu_ref[...])


def _ev_in(h, g, w_in, w_a2, b_a):
    d = h.shape[1]
    nqk = 2 * GLA_HEADS * GLA_DK
    nv = GLA_HEADS * GLA_DV
    n_main = nqk + 2 * nv
    w_main = w_in[:, :n_main]
    w_a = jnp.pad(w_in[:, n_main:n_main + GLA_RANK], ((0, 0), (0, LANES - GLA_RANK)))
    w_u = w_in[:, n_main + GLA_RANK:]
    w_a2p = jnp.pad(w_a2, ((0, LANES - GLA_RANK), (0, 0)))
    consts = [g.reshape(1, d), w_main.astype(BF16), w_a.astype(BF16), w_a2p.astype(BF16),
              b_a.reshape(1, -1), w_u.astype(BF16)]
    outs = [(nqk, F32), (nv, F32), (nv, F32), (GLA_HEADS * GLA_DK, F32), (w_u.shape[1], F32)]
    return _row_tiled(_ev_in_body, [h], consts, outs, ROW_TILE, "ev_in")


def _gla_body(qk_ref, v_ref, r_ref, la_ref, gn_ref, o_ref, st_ref):
    tl = qk_ref.shape[0]
    hdk = GLA_HEADS * GLA_DK
    c = GLA_CHUNK

    @pl.when(pl.program_id(1) == 0)
    def _():
        st_ref[...] = jnp.zeros_like(st_ref)

    row = lax.broadcasted_iota(I32, (tl, hdk), 0) % c
    bc = la_ref[...]
    d = 1
    while d < c:
        bc = bc + jnp.where(row >= d, pltpu.roll(bc, d, axis=0), 0.0)
        d *= 2
    q_dec = qk_ref[:, :hdk] * (GLA_DK ** -0.5) * jnp.exp(bc)
    k = qk_ref[:, hdk:]
    k_inv = k * jnp.exp(-bc)
    causal = lax.broadcasted_iota(I32, (c, c), 0) >= lax.broadcasted_iota(I32, (c, c), 1)
    gn = gn_ref[...]
    for j in range(tl // c):
        rs = slice(j * c, (j + 1) * c)
        b_last = bc[j * c + c - 1:j * c + c, :]
        k_end = k[rs] * jnp.exp(b_last - bc[rs])
        dec = jnp.exp(b_last)
        for h in range(GLA_HEADS):
            ks = slice(h * GLA_DK, (h + 1) * GLA_DK)
            vs = slice(h * GLA_DV, (h + 1) * GLA_DV)
            qd = q_dec[rs, ks].astype(BF16)
            vh = v_ref[rs, vs].astype(BF16)
            s = lax.dot_general(qd, k_inv[rs, ks].astype(BF16), _NT, preferred_element_type=F32)
            s = jnp.where(causal, s, 0.0).astype(BF16)
            st = st_ref[h]
            o = _dot(s, vh) + lax.dot_general(qd, st.astype(BF16), _NT, preferred_element_type=F32)
            st_ref[h] = st * dec[:, ks] + lax.dot_general(
                vh, k_end[:, ks].astype(BF16), _TN, preferred_element_type=F32)
            o = o * lax.rsqrt(jnp.mean(o * o, axis=-1, keepdims=True) + EPS) * gn
            r = r_ref[rs, vs]
            o_ref[rs, vs] = o * (r * jax.nn.sigmoid(r))


def _gla(qk, v, r, la, gn):
    b, l, _ = qk.shape
    tl = GLA_TILE

    def spec(w):
        return pl.BlockSpec((None, tl, w), lambda bi, t: (bi, t, 0))

    return pl.pallas_call(
        _gla_body,
        grid=(b, l // tl),
        in_specs=[spec(qk.shape[2]), spec(v.shape[2]), spec(r.shape[2]), spec(la.shape[2]),
                  pl.BlockSpec((1, GLA_DV), lambda bi, t: (0, 0))],
        out_specs=spec(v.shape[2]),
        out_shape=jax.ShapeDtypeStruct(v.shape, F32),
        scratch_shapes=[pltpu.VMEM((GLA_HEADS, GLA_DV, GLA_DK), F32)],
        compiler_params=_params("parallel", "arbitrary"),
        name="gla",
    )(qk, v, r, la, gn.reshape(1, GLA_DV))


def _cmul(ar, ai, br, bi):
    return ar * br - ai * bi, ar * bi + ai * br


def _s5_prep_body(lr_ref, li_ref, ldt_ref, br_ref, bi_ref, bbr_ref, bbi_ref, pr_ref, pi_ref):
    lr = jnp.minimum(lr_ref[...], -1e-4)
    li = li_ref[...]
    dt = jnp.exp(ldt_ref[...])
    mag = jnp.exp(lr * dt)
    ar = mag * jnp.cos(li * dt)
    ai = mag * jnp.sin(li * dt)
    den = lr * lr + li * li
    nr = ar - 1.0
    fr = (nr * lr + ai * li) / den
    fi = (ai * lr - nr * li) / den
    w = br_ref.shape[2]
    for j in range(br_ref.shape[0]):
        sl = slice(j * w, (j + 1) * w)
        b_re, b_im = br_ref[j], bi_ref[j]
        bbr_ref[j] = (fr[:, sl] * b_re - fi[:, sl] * b_im).astype(BF16)
        bbi_ref[j] = (fr[:, sl] * b_im + fi[:, sl] * b_re).astype(BF16)
    pr_ref[0:1, :] = ar
    pi_ref[0:1, :] = ai
    n = 1
    while n < pr_ref.shape[0]:
        sr, si = pr_ref[n - 1:n, :], pi_ref[n - 1:n, :]
        nr_, ni_ = _cmul(pr_ref[0:n, :], pi_ref[0:n, :], sr, si)
        pr_ref[n:2 * n, :] = nr_
        pi_ref[n:2 * n, :] = ni_
        n *= 2


def _s5_body(u_ref, bbr_ref, bbi_ref, pr_ref, pi_ref, cr_ref, ci_ref, d_ref, wg_ref, bg_ref,
             o_ref, sr_ref, si_ref):
    t = u_ref.shape[0]
    w = bbr_ref.shape[2]
    cw = bbr_ref.shape[1]

    @pl.when(pl.program_id(1) == 0)
    def _():
        sr_ref[...] = jnp.zeros_like(sr_ref)
        si_ref[...] = jnp.zeros_like(si_ref)

    u = u_ref[...]
    ub = u.astype(BF16)
    row = lax.broadcasted_iota(I32, (t, w), 0)
    ys = []
    for j in range(bbr_ref.shape[0]):
        ls = slice(j * w, (j + 1) * w)
        uj = ub[:, j * cw:(j + 1) * cw]
        xr = _dot(uj, bbr_ref[j])
        xi = _dot(uj, bbi_ref[j])
        d = 1
        while d < t:
            ar, ai = pr_ref[d - 1:d, ls], pi_ref[d - 1:d, ls]
            if d < 8:
                shr = jnp.where(row >= d, pltpu.roll(xr, d, axis=0), 0.0)
                shi = jnp.where(row >= d, pltpu.roll(xi, d, axis=0), 0.0)
            else:
                z = jnp.zeros((d, w), F32)
                shr = jnp.concatenate([z, xr[:t - d]], axis=0)
                shi = jnp.concatenate([z, xi[:t - d]], axis=0)
            dr, di = _cmul(ar, ai, shr, shi)
            xr, xi = xr + dr, xi + di
            d *= 2
        dr, di = _cmul(pr_ref[:, ls], pi_ref[:, ls], sr_ref[:, ls], si_ref[:, ls])
        xr, xi = xr + dr, xi + di
        sr_ref[:, ls] = xr[t - 1:t]
        si_ref[:, ls] = xi[t - 1:t]
        ys.append(_dot(xr.astype(BF16), cr_ref[j]) - _dot(xi.astype(BF16), ci_ref[j]))
    y = jnp.concatenate(ys, axis=1) + d_ref[...] * u
    y = jax.nn.gelu(y)
    gate = jax.nn.sigmoid(_dot(y.astype(BF16), wg_ref[...]) + bg_ref[...])
    o_ref[...] = y * gate


def _block_diag(m, split):
    g, r, c = m.shape
    gs = g // split
    eye = jnp.eye(gs, dtype=m.dtype)
    m = m.reshape(split, gs, r, c)
    out = m[:, :, :, None, :] * eye[None, :, None, :, None]
    return out.reshape(split, gs * r, gs * c)


def _s5(u, lam_re, lam_im, log_dt, b_re, b_im, c_re, c_im, dd, w_glu, b_glu):
    b, l, width = u.shape
    g, p = lam_re.shape
    n_state = g * p
    t = S5_TILE
    br_bd = _block_diag(jnp.swapaxes(b_re, 1, 2), S5_SPLIT)
    bi_bd = _block_diag(jnp.swapaxes(b_im, 1, 2), S5_SPLIT)
    cr_bd = _block_diag(jnp.swapaxes(c_re, 1, 2), S5_SPLIT).astype(BF16)
    ci_bd = _block_diag(jnp.swapaxes(c_im, 1, 2), S5_SPLIT).astype(BF16)
    ldt = jnp.broadcast_to(log_dt[:, None], (g, p)).reshape(1, n_state)
    bbr, bbi, pw_r, pw_i = pl.pallas_call(
        _s5_prep_body,
        out_shape=[jax.ShapeDtypeStruct(br_bd.shape, BF16), jax.ShapeDtypeStruct(br_bd.shape, BF16),
                   jax.ShapeDtypeStruct((t, n_state), F32), jax.ShapeDtypeStruct((t, n_state), F32)],
        compiler_params=pltpu.CompilerParams(vmem_limit_bytes=VMEM_LIMIT),
        name="s5_prep",
    )(lam_re.reshape(1, n_state), lam_im.reshape(1, n_state), ldt, br_bd, bi_bd)

    consts = [bbr, bbi, pw_r, pw_i, cr_bd, ci_bd, dd.reshape(1, width), w_glu.astype(BF16),
              b_glu.reshape(1, width)]
    return pl.pallas_call(
        _s5_body,
        grid=(b, l // t),
        in_specs=[pl.BlockSpec((None, t, width), lambda bi_, ti: (bi_, ti, 0))]
        + [_const_spec(c) for c in consts],
        out_specs=pl.BlockSpec((None, t, width), lambda bi_, ti: (bi_, ti, 0)),
        out_shape=jax.ShapeDtypeStruct(u.shape, F32),
        scratch_shapes=[pltpu.VMEM((1, n_state), F32), pltpu.VMEM((1, n_state), F32)],
        compiler_params=_params("parallel", "arbitrary"),
        name="s5",
    )(u, *consts)


def _rope(x, cos, sin_signed, half):
    lane = lax.broadcasted_iota(I32, x.shape, 1) % (2 * half)
    rot = jnp.where(lane < half, pltpu.roll(x, LANES - half, axis=1), pltpu.roll(x, half, axis=1))
    return x * cos + rot * sin_signed


def _od_in_body(h_ref, ca_ref, sa_ref, ci_ref, si_ref, g_ref, wm_ref, wk_ref, ww_ref,
                q_ref, k_ref, v_ref, qi_ref, ki_ref, wi_ref):
    n = _rms(h_ref[...], g_ref[...]).astype(BF16)
    main = _dot(n, wm_ref[...])
    ca, sa, ci, si = ca_ref[...], sa_ref[...], ci_ref[...], si_ref[...]
    hd = ATT_HEAD_DIM
    nq, nk = q_ref.shape[0] * hd, k_ref.shape[1]
    for h in range(nq // hd):
        sl = slice(h * hd, (h + 1) * hd)
        q_ref[h] = (_rope(main[:, sl], ca, sa, hd // 2) * (hd ** -0.5)).astype(BF16)
    for h in range(nk // hd):
        k_ref[:, h * hd:(h + 1) * hd] = _rope(
            main[:, nq + h * hd:nq + (h + 1) * hd], ca, sa, hd // 2).astype(BF16)
    v_ref[...] = main[:, nq + nk:nq + 2 * nk].astype(BF16)
    o = nq + 2 * nk
    first = lax.broadcasted_iota(I32, (h_ref.shape[0], LANES), 1) < IDX_DIM
    for s in range(IDX_HEADS * IDX_DIM // LANES):
        pair = _rope(main[:, o + s * LANES:o + (s + 1) * LANES], ci, si, IDX_DIM // 2)
        qi_ref[:, 2 * s * LANES:(2 * s + 1) * LANES] = jnp.where(first, pair, 0.0).astype(BF16)
        qi_ref[:, (2 * s + 1) * LANES:(2 * s + 2) * LANES] = jnp.where(
            first, pltpu.roll(pair, IDX_DIM, axis=1), 0.0).astype(BF16)
    ki_ref[...] = _rope(_dot(n, wk_ref[...]), ci, si, IDX_DIM // 2).astype(BF16)
    wi_ref[...] = _dot(n, ww_ref[...]) * (IDX_HEADS ** -0.5 * IDX_DIM ** -0.5)


def _rope_tables(positions, dim):
    inv = ROPE_THETA ** (-jnp.arange(0, dim, 2, dtype=F32) / dim)
    ang = positions.astype(F32)[..., None] * inv
    cos = jnp.cos(ang)
    sin = jnp.sin(ang)
    reps = LANES // dim
    cos = jnp.tile(jnp.concatenate([cos, cos], -1), (1, 1, reps))
    sin = jnp.tile(jnp.concatenate([-sin, sin], -1), (1, 1, reps))
    return cos.reshape(-1, LANES), sin.reshape(-1, LANES)


def _od_in(h, g, w_in, tables):
    d = h.shape[1]
    nq = ATT_HEADS * ATT_HEAD_DIM
    nk = ATT_KV_HEADS * ATT_HEAD_DIM
    nqi = IDX_HEADS * IDX_DIM
    n_main = nq + 2 * nk + nqi
    w_main = w_in[:, :n_main]
    w_ki = jnp.pad(w_in[:, n_main:n_main + IDX_DIM], ((0, 0), (0, LANES - IDX_DIM)))
    w_wi = jnp.pad(w_in[:, n_main + IDX_DIM:], ((0, 0), (0, LANES - IDX_HEADS)))
    consts = [g.reshape(1, d), w_main.astype(BF16), w_ki.astype(BF16), w_wi.astype(BF16)]
    outs = [(ATT_HEADS, ATT_HEAD_DIM, BF16), (nk, BF16), (nk, BF16), (IDX_HEADS * LANES, BF16),
            (LANES, BF16), (LANES, F32)]
    return _row_tiled(_od_in_body, [h] + list(tables), consts, outs, ROW_TILE, "od_in")


def _sort_key(x):
    bits = lax.bitcast_convert_type(x, I32)
    return bits ^ ((bits >> 31) & jnp.int32(0x7FFFFFFF))


def _key_value(k):
    return lax.bitcast_convert_type(k ^ ((k >> 31) & jnp.int32(0x7FFFFFFF)), F32)


def _slab_reduce(x, op):
    ways = 4
    parts = [x[r * 8:(r + 1) * 8] for r in range(ways)]
    for r in range(ways, x.shape[0] // 8):
        parts[r % ways] = op(parts[r % ways], x[r * 8:(r + 1) * 8])
    return op(op(parts[0], parts[1]), op(parts[2], parts[3]))


def _sublane_all(x, op):
    for s in (4, 2, 1):
        x = op(x, pltpu.roll(x, s, axis=0))
    return x


def _dsa_body(qi_ref, wi_ref, q_ref, ki_ref, k_ref, vt_ref, o_ref,
              sc_ref, m_ref, l_ref, acc_ref, *, topk):
    tq = q_ref.shape[1]
    tk = sc_ref.shape[1]
    hd = ATT_HEAD_DIM
    group = ATT_HEADS // ATT_KV_HEADS
    r0 = pl.program_id(1) * tq
    nkb = (r0 + tq + tk - 1) // tk
    qpos = r0 + lax.broadcasted_iota(I32, (1, tq), 1)
    key_ninf = jnp.int32(-2139095041)
    key_lo0 = key_ninf + 1

    def kpos_of(kb):
        return kb * tk + lax.broadcasted_iota(I32, (tk, 1), 0)

    wit = wi_ref[...].T

    def score_block(kb, ext):
        kmax, kmin = ext
        c0 = pl.multiple_of(kb * tk, tk)
        kib = ki_ref[pl.ds(c0, tk), :]
        acc = jnp.zeros((tk, tq), F32)
        for h in range(IDX_HEADS):
            lg = lax.dot_general(kib, qi_ref[:, h * LANES:(h + 1) * LANES], _NT,
                                 preferred_element_type=F32)
            acc = acc + jnp.maximum(lg, 0.0) * wit[h:h + 1, :]
        acc = jnp.where(acc == 0.0, 0.0, acc)
        valid = kpos_of(kb) <= qpos
        key = _sort_key(acc)
        keys = jnp.where(valid, key, key_ninf)
        sc_ref[kb] = keys
        return (jnp.maximum(kmax, _slab_reduce(keys, jnp.maximum)),
                jnp.minimum(kmin, _slab_reduce(jnp.where(valid, key, int_max), jnp.minimum)))

    int_max = jnp.int32(2147483647)
    kmax, kmin = lax.fori_loop(0, nkb, score_block, (jnp.full((8, tq), key_ninf, I32),
                                                    jnp.full((8, tq), int_max, I32)))
    kmax = _sublane_all(kmax, jnp.maximum)[0:1]
    kmin = _sublane_all(kmin, jnp.minimum)[0:1]

    def count(pred):
        def body(kb, cnt):
            return cnt + _slab_reduce(jnp.where(pred(sc_ref[kb], kb), 1.0, 0.0), jnp.add)
        cnt = lax.fori_loop(0, nkb, body, jnp.zeros((8, tq), F32))
        return _sublane_all(cnt, jnp.add)[0:1]

    kf = jnp.float32(topk)
    short = (qpos + 1) <= topk

    def search_cond(st):
        return jnp.logical_and(st[0] < 200, st[-1] > 0.0)

    def search_body(st):
        it, lo, hi, c_lo, c_hi, n_acc, side, streak, _ = st
        mixed = jnp.logical_and(lo < 0, hi > 0)
        w = hi - lo
        active = jnp.logical_not(jnp.logical_or(
            short, jnp.logical_or(c_lo == kf, jnp.logical_and(jnp.logical_not(mixed), w <= 1))))
        gal = lax.shift_left(jnp.int32(1 << 22), jnp.clip(it - 1, 0, 8))
        frac = (c_lo - kf - 0.5) / (c_lo - c_hi)
        d_key = (w.astype(F32) * frac).astype(I32)
        v_lo, v_hi = _key_value(lo), _key_value(hi)
        d_val = _sort_key(v_lo + (v_hi - v_lo) * frac) - lo
        neg = hi <= 0
        d = jnp.where(streak >= 2, w >> 1, jnp.where(neg, d_val, d_key))
        d = jnp.where(jnp.logical_and(n_acc == 0, jnp.logical_not(neg)),
                      w - jnp.minimum(gal, w - 1), d)
        cand = jnp.where(mixed, 0, lo + jnp.clip(d, 1, jnp.maximum(w - 1, 1)))
        cnt = count(lambda keys, kb: keys >= cand)
        up = jnp.logical_and(active, cnt >= kf)
        dn = jnp.logical_and(active, cnt < kf)
        lo = jnp.where(up, cand, lo)
        c_lo = jnp.where(up, cnt, c_lo)
        hi = jnp.where(dn, cand, hi)
        c_hi = jnp.where(dn, cnt, c_hi)
        n_acc = n_acc + jnp.where(jnp.logical_and(up, jnp.logical_not(mixed)), 1, 0)
        new_side = jnp.where(up, 1, -1)
        streak = jnp.where(new_side == side, streak + 1, 1)
        mixed = jnp.logical_and(lo < 0, hi > 0)
        settled = jnp.logical_or(
            short, jnp.logical_or(c_lo == kf, jnp.logical_and(jnp.logical_not(mixed), hi - lo <= 1)))
        pending = jnp.sum(jnp.where(settled, 0.0, 1.0))
        return it + 1, lo, hi, c_lo, c_hi, n_acc, new_side, streak, pending

    zi = jnp.zeros((1, tq), I32)
    st0 = (jnp.int32(0), kmin, kmax + 1, (qpos + 1).astype(F32), jnp.zeros((1, tq), F32),
           zi, zi, zi, jnp.float32(1.0))
    st = lax.while_loop(search_cond, search_body, st0)
    thr = jnp.where(short, key_lo0, st[1])
    cnt_thr = st[3]

    tied = jnp.logical_and(cnt_thr > kf, jnp.logical_not(short))
    n_tied = jnp.sum(jnp.where(tied, 1.0, 0.0))

    @pl.when(n_tied > 0.0)
    def _():
        need = kf - st[4]
        ltri = jnp.where(lax.broadcasted_iota(I32, (tk, tk), 0) >= lax.broadcasted_iota(I32, (tk, tk), 1),
                         1.0, 0.0).astype(BF16)

        def demote(kb, before):
            keys = sc_ref[kb]
            is_tie = keys == thr
            rank = _dot(ltri, jnp.where(is_tie, 1.0, 0.0).astype(BF16)) + before
            drop = jnp.logical_and(jnp.logical_and(is_tie, rank > need), tied)
            sc_ref[kb] = jnp.where(drop, key_ninf, keys)
            return rank[tk - 1:tk, :]

        lax.fori_loop(0, nkb, demote, jnp.zeros((1, tq), F32))

    def to_bias(kb, carry):
        bias = jnp.where(sc_ref[kb] >= thr, 0.0, NEG).astype(F32)
        sc_ref[kb] = lax.bitcast_convert_type(bias, I32)
        return carry

    lax.fori_loop(0, nkb, to_bias, 0)

    m_ref[...] = jnp.full_like(m_ref, NEG)
    l_ref[...] = jnp.zeros_like(l_ref)
    acc_ref[...] = jnp.zeros_like(acc_ref)

    def attend(kb, carry):
        c0 = pl.multiple_of(kb * tk, tk)
        bias = lax.bitcast_convert_type(sc_ref[kb], F32)
        bias = jnp.concatenate([bias] * group, axis=1)
        for kv in range(ATT_KV_HEADS):
            kblk = k_ref[pl.ds(c0, tk), kv * hd:(kv + 1) * hd]
            qg = q_ref[kv * group:(kv + 1) * group].reshape(group * tq, hd)
            s = lax.dot_general(kblk, qg, _NT, preferred_element_type=F32) + bias
            m_old = m_ref[kv:kv + 1, :]
            m_new = jnp.maximum(m_old, jnp.max(s, axis=0, keepdims=True))
            alpha = jnp.exp(m_old - m_new)
            p = jnp.exp(s - m_new)
            l_ref[kv:kv + 1, :] = alpha * l_ref[kv:kv + 1, :] + jnp.sum(p, axis=0, keepdims=True)
            acc_ref[kv] = alpha * acc_ref[kv] + _dot(vt_ref[kb, kv * hd:(kv + 1) * hd, :],
                                                     p.astype(BF16))
            m_ref[kv:kv + 1, :] = m_new
        return carry

    lax.fori_loop(0, nkb, attend, 0)
    for h in range(ATT_HEADS):
        kv, cs = h // group, slice((h % group) * tq, (h % group + 1) * tq)
        o_ref[:, h * hd:(h + 1) * hd] = (acc_ref[kv][:, cs] / l_ref[kv:kv + 1, cs]).T.astype(o_ref.dtype)


def _dsa(q, k, v, qi, ki, wi):
    b, l, nv = v.shape
    nh, _, hd = q.shape
    tq, tk = min(DSA_TQ, l), min(DSA_TK, l)
    topk = min(TOPK_MAX, l // 4)
    group = nh // ATT_KV_HEADS
    vt = jnp.swapaxes(v.reshape(b, l // tk, tk, nv), 2, 3)

    def qspec(w):
        return pl.BlockSpec((None, tq, w), lambda bi, i: (bi, i, 0))

    def kspec(w):
        return pl.BlockSpec((None, l, w), lambda bi, i: (bi, 0, 0))

    return pl.pallas_call(
        functools.partial(_dsa_body, topk=topk),
        grid=(b, l // tq),
        in_specs=[qspec(qi.shape[2]), qspec(wi.shape[2]),
                  pl.BlockSpec((nh, tq, hd), lambda bi, i: (0, bi * (l // tq) + i, 0)),
                  kspec(ki.shape[2]), kspec(k.shape[2]),
                  pl.BlockSpec((None, l // tk, nv, tk), lambda bi, i: (bi, 0, 0, 0))],
        out_specs=qspec(nh * hd),
        out_shape=jax.ShapeDtypeStruct((b, l, nh * hd), BF16),
        scratch_shapes=[pltpu.VMEM((l // tk, tk, tq), I32),
                        pltpu.VMEM((ATT_KV_HEADS, group * tq), F32),
                        pltpu.VMEM((ATT_KV_HEADS, group * tq), F32),
                        pltpu.VMEM((ATT_KV_HEADS, hd, group * tq), F32)],
        compiler_params=_params("parallel", "arbitrary"),
        name="dsa",
    )(qi, wi, q, ki, k, vt)


def _xattn_body(h_ref, g_ref, wq_ref, kv_ref, wo_ref, o_ref):
    x = h_ref[...]
    d = x.shape[1]
    hd = d // XA_HEADS
    n = _rms(x, g_ref[...]).astype(BF16)
    q = (_dot(n, wq_ref[...]) * (hd ** -0.5)).astype(BF16)
    outs = []
    for h in range(XA_HEADS):
        hs = slice(h * hd, (h + 1) * hd)
        s = lax.dot_general(q[:, hs], kv_ref[:, hs], _NT, preferred_element_type=F32)
        p = jnp.exp(s - jnp.max(s, axis=1, keepdims=True))
        p = p / jnp.sum(p, axis=1, keepdims=True)
        outs.append(_dot(p.astype(BF16), kv_ref[:, d + h * hd:d + (h + 1) * hd]))
    o = jnp.concatenate(outs, axis=1).astype(BF16)
    o_ref[...] = x + _dot(o, wo_ref[...])


def _xattn(h, g, wq, kv, wo, rows_per_batch):
    m, d = h.shape
    tm = ROW_TILE
    per = rows_per_batch // tm
    consts = [g.reshape(1, d), wq.astype(BF16)]
    wo_b = wo.astype(BF16)
    return pl.pallas_call(
        _xattn_body,
        grid=(m // tm,),
        in_specs=[pl.BlockSpec((tm, d), lambda i: (i, 0))] + [_const_spec(c) for c in consts]
        + [pl.BlockSpec((None,) + kv.shape[1:], lambda i: (i // per, 0, 0)), _const_spec(wo_b)],
        out_specs=pl.BlockSpec((tm, d), lambda i: (i, 0)),
        out_shape=jax.ShapeDtypeStruct((m, d), F32),
        compiler_params=_params("parallel"),
        name="xattn",
    )(h, *consts, kv, wo_b)


def kernel(x, mem, positions, ffn1_norm, ffn1_w_gate, ffn1_w_up, ffn1_w_down, mix_norm, ev_w_in, ev_w_out, gla_w_alpha, gla_b_alpha, gla_norm, s5_lambda_re, s5_lambda_im, s5_log_dt, s5_B_re, s5_B_im, s5_C_re, s5_C_im, s5_D, s5_w_glu, s5_b_glu, od_w_in, od_w_out, xa_norm, mem_norm, xa_wq, xa_wk, xa_wv, xa_wo, ffn2_norm, ffn2_w_gate, ffn2_w_up, ffn2_w_down, final_norm):
    b, l, d = x.shape
    n_mem = mem.shape[1]
    depth = ffn1_norm.shape[0]
    tables = _rope_tables(positions, ATT_HEAD_DIM) + _rope_tables(positions, IDX_DIM)
    mem2 = mem.reshape(b * n_mem, d)
    h = x.reshape(b * l, d)

    def seq(a):
        return a.reshape(b, l, a.shape[-1])

    def flat(a):
        return a.reshape(b * l, a.shape[-1])

    for layer in range(depth):
        h = _ffn(h, ffn1_norm[layer], ffn1_w_gate[layer], ffn1_w_up[layer], ffn1_w_down[layer])
        if layer % 2 == 0:
            e = layer // 2
            qk, v, r, la, u = _ev_in(h, mix_norm[layer], ev_w_in[e], gla_w_alpha[e], gla_b_alpha[e])
            o_gla = _gla(seq(qk), seq(v), seq(r), seq(la), gla_norm[e])
            o_s5 = _s5(seq(u), s5_lambda_re[e], s5_lambda_im[e], s5_log_dt[e], s5_B_re[e],
                       s5_B_im[e], s5_C_re[e], s5_C_im[e], s5_D[e], s5_w_glu[e], s5_b_glu[e])
            gw = o_gla.shape[-1]
            h = _proj_res(h, [flat(o_gla), flat(o_s5)], [ev_w_out[e][:gw], ev_w_out[e][gw:]])
        else:
            o = layer // 2
            q, k, v, qi, ki, wi = _od_in(h, mix_norm[layer], od_w_in[o], tables)
            att = _dsa(q, seq(k), seq(v), seq(qi), seq(ki), seq(wi))
            h = _proj_res(h, [flat(att)], [od_w_out[o]])
        kv = _norm_proj(mem2, mem_norm, jnp.concatenate([xa_wk[layer], xa_wv[layer]], axis=1),
                        n_mem, BF16)
        h = _xattn(h, xa_norm[layer], xa_wq[layer], kv.reshape(b, n_mem, 2 * d), xa_wo[layer], l)
        last = layer == depth - 1
        h = _ffn(h, ffn2_norm[layer], ffn2_w_gate[layer], ffn2_w_up[layer], ffn2_w_down[layer],
                 final_g=final_norm if last else None)
    return h.reshape(b, l, d)
```

```python
import functools
import math

import jax
import jax.numpy as jnp
from jax import lax
from jax.experimental import pallas as pl
from jax.experimental.pallas import tpu as pltpu

F32 = jnp.float32
BF16 = jnp.bfloat16
I32 = jnp.int32

EPS = 1e-6
ROPE_THETA = 10000.0
GLA_HEADS, GLA_DK, GLA_DV, GLA_RANK, GLA_TAU, GLA_CHUNK = 4, 64, 128, 16, 16.0, 64
S5_GROUP, S5_STATE = 16, 64
ATT_HEADS, ATT_KV_HEADS, ATT_HEAD_DIM = 8, 2, 128
IDX_HEADS, IDX_DIM = 8, 64
TOPK_MAX = 256
XA_HEADS = 4

LANES = 128
VMEM_LIMIT = 56 * 1024 * 1024
ROW_TILE = 512
FFN_CHUNK = 256
GLA_TILE = 256
S5_TILE = 256
S5_SPLIT = 4
DSA_TQ = 256
DSA_TK = 512
NEG = -(2.0 ** 100)
SUBLANES_BF16 = 16

_NT = (((1,), (1,)), ((), ()))
_TN = (((0,), (0,)), ((), ()))


def _params(*sem):
    return pltpu.CompilerParams(dimension_semantics=sem, vmem_limit_bytes=VMEM_LIMIT)


def _const_spec(a):
    if isinstance(a, tuple):
        a, layer = a
        nd = a.ndim - 1
        return pl.BlockSpec((None,) + a.shape[1:], lambda *_: (layer,) + (0,) * nd,
                            pipeline_mode=pl.Buffered(1))
    nd = a.ndim
    return pl.BlockSpec(a.shape, lambda *_: (0,) * nd, pipeline_mode=pl.Buffered(1))


def _rms(x, g):
    return x * lax.rsqrt(jnp.mean(x * x, axis=-1, keepdims=True) + EPS) * g


def _dot(a, b):
    return jnp.dot(a, b, preferred_element_type=F32)


def _row_tiled(body, rows, consts, outs, tm, name):
    m = rows[0].shape[0]
    assert m % tm == 0

    def out_desc(o):
        if len(o) == 2:
            return pl.BlockSpec((tm, o[0]), lambda i: (i, 0)), jax.ShapeDtypeStruct((m, o[0]), o[1])
        return (pl.BlockSpec((o[0], tm, o[1]), lambda i: (0, i, 0)),
                jax.ShapeDtypeStruct((o[0], m, o[1]), o[2]))

    descs = [out_desc(o) for o in outs]
    return pl.pallas_call(
        body,
        grid=(m // tm,),
        in_specs=[pl.BlockSpec((tm, r.shape[1]), lambda i: (i, 0)) for r in rows]
        + [_const_spec(c) for c in consts],
        out_specs=[d[0] for d in descs],
        out_shape=[d[1] for d in descs],
        compiler_params=_params("parallel"),
        name=name,
    )(*rows, *[c[0] if isinstance(c, tuple) else c for c in consts])


def _ffn_body(h_ref, g_ref, wg_ref, wu_ref, wd_ref, *rest, final):
    o_ref = rest[-1]
    x = h_ref[...]
    n = _rms(x, g_ref[...]).astype(BF16)
    acc = jnp.zeros_like(x)
    for c in range(wg_ref.shape[1] // FFN_CHUNK):
        sl = slice(c * FFN_CHUNK, (c + 1) * FFN_CHUNK)
        g = _dot(n, wg_ref[:, sl].astype(BF16))
        u = _dot(n, wu_ref[:, sl].astype(BF16))
        a = (g * jax.nn.sigmoid(g) * u).astype(BF16)
        acc = acc + _dot(a, wd_ref[sl, :].astype(BF16))
    y = x + 0.5 * acc
    if final:
        y = _rms(y, rest[0][...])
    o_ref[...] = y


def _ffn(h, g, wg, wu, wd, layer, final_g=None):
    d = h.shape[1]
    consts = [g.reshape(1, d), (wg, layer), (wu, layer), (wd, layer)]
    if final_g is not None:
        consts.append(final_g.reshape(1, d))
    body = functools.partial(_ffn_body, final=final_g is not None)
    return _row_tiled(body, [h], consts, [(d, F32)], ROW_TILE, "ffn")[0]


def _proj_res_body(*refs):
    h_ref, w_ref, o_ref = refs[0], refs[-2], refs[-1]
    acc = h_ref[...]
    r0 = 0
    for x_ref in refs[1:-2]:
        k = x_ref.shape[1]
        acc = acc + _dot(x_ref[...].astype(BF16), w_ref[r0:r0 + k, :].astype(BF16))
        r0 += k
    o_ref[...] = acc


def _proj_res(h, xs, w_stack, layer):
    return _row_tiled(_proj_res_body, [h] + list(xs), [(w_stack, layer)],
                      [(h.shape[1], F32)], ROW_TILE, "proj_res")[0]


def _norm_proj_body(x_ref, g_ref, w_ref, o_ref):
    n = _rms(x_ref[...], g_ref[...]).astype(BF16)
    o_ref[...] = _dot(n, w_ref[...]).astype(o_ref.dtype)


def _norm_proj(x, g, w, tm, out_dtype):
    return _row_tiled(_norm_proj_body, [x], [g.reshape(1, -1), w.astype(BF16)],
                      [(w.shape[1], out_dtype)], tm, "norm_proj")[0]


def _log_sigmoid(x):
    return jnp.minimum(x, 0.0) - jnp.log(1.0 + jnp.exp(-jnp.abs(x)))


def _ev_in_body(h_ref, g_ref, wm_ref, wa_ref, wa2_ref, ba_ref, wu_ref,
                qk_ref, v_ref, r_ref, la_ref, u_ref):
    n = _rms(h_ref[...], g_ref[...]).astype(BF16)
    nqk = qk_ref.shape[1]
    nv = v_ref.shape[1]
    main = _dot(n, wm_ref[...])
    qk_ref[...] = main[:, :nqk]
    v_ref[...] = main[:, nqk:nqk + nv]
    r_ref[...] = main[:, nqk + nv:]
    a_low = _dot(n, wa_ref[...]).astype(BF16)
    alpha = _dot(a_low, wa2_ref[...]) + ba_ref[...]
    la_ref[...] = _log_sigmoid(alpha) / GLA_TAU
    u_ref[...] = _dot(n, wu_ref[...])


def _ev_in(h, g, w_in, w_a2, b_a):
    d = h.shape[1]
    nqk = 2 * GLA_HEADS * GLA_DK
    nv = GLA_HEADS * GLA_DV
    n_main = nqk + 2 * nv
    w_main = w_in[:, :n_main]
    w_a = jnp.pad(w_in[:, n_main:n_main + GLA_RANK], ((0, 0), (0, LANES - GLA_RANK)))
    w_u = w_in[:, n_main + GLA_RANK:]
    w_a2p = jnp.pad(w_a2, ((0, LANES - GLA_RANK), (0, 0)))
    consts = [g.reshape(1, d), w_main.astype(BF16), w_a.astype(BF16), w_a2p.astype(BF16),
              b_a.reshape(1, -1), w_u.astype(BF16)]
    outs = [(nqk, F32), (nv, F32), (nv, F32), (GLA_HEADS * GLA_DK, F32), (w_u.shape[1], F32)]
    return _row_tiled(_ev_in_body, [h], consts, outs, ROW_TILE, "ev_in")


def _gla_body(qk_ref, v_ref, r_ref, la_ref, gn_ref, o_ref, st_ref):
    tl = qk_ref.shape[0]
    hdk = GLA_HEADS * GLA_DK
    c = GLA_CHUNK

    @pl.when(pl.program_id(1) == 0)
    def _():
        st_ref[...] = jnp.zeros_like(st_ref)

    row = lax.broadcasted_iota(I32, (tl, hdk), 0) % c
    bc = la_ref[...]
    d = 1
    while d < c:
        bc = bc + jnp.where(row >= d, pltpu.roll(bc, d, axis=0), 0.0)
        d *= 2
    q_dec = qk_ref[:, :hdk] * (GLA_DK ** -0.5) * jnp.exp(bc)
    k = qk_ref[:, hdk:]
    k_inv = k * jnp.exp(-bc)
    causal = lax.broadcasted_iota(I32, (c, c), 0) >= lax.broadcasted_iota(I32, (c, c), 1)
    gn = gn_ref[...]
    for j in range(tl // c):
        rs = slice(j * c, (j + 1) * c)
        b_last = bc[j * c + c - 1:j * c + c, :]
        k_end = k[rs] * jnp.exp(b_last - bc[rs])
        dec = jnp.exp(b_last)
        for h in range(GLA_HEADS):
            ks = slice(h * GLA_DK, (h + 1) * GLA_DK)
            vs = slice(h * GLA_DV, (h + 1) * GLA_DV)
            qd = q_dec[rs, ks].astype(BF16)
            vh = v_ref[rs, vs].astype(BF16)
            s = lax.dot_general(qd, k_inv[rs, ks].astype(BF16), _NT, preferred_element_type=F32)
            s = jnp.where(causal, s, 0.0).astype(BF16)
            st = st_ref[h]
            o = _dot(s, vh) + lax.dot_general(qd, st.astype(BF16), _NT, preferred_element_type=F32)
            st_ref[h] = st * dec[:, ks] + lax.dot_general(
                vh, k_end[:, ks].astype(BF16), _TN, preferred_element_type=F32)
            o = o * lax.rsqrt(jnp.mean(o * o, axis=-1, keepdims=True) + EPS) * gn
            r = r_ref[rs, vs]
            o_ref[rs, vs] = o * (r * jax.nn.sigmoid(r))


def _gla(qk, v, r, la, gn):
    b, l, _ = qk.shape
    tl = GLA_TILE

    def spec(w):
        return pl.BlockSpec((None, tl, w), lambda bi, t: (bi, t, 0))

    return pl.pallas_call(
        _gla_body,
        grid=(b, l // tl),
        in_specs=[spec(qk.shape[2]), spec(v.shape[2]), spec(r.shape[2]), spec(la.shape[2]),
                  pl.BlockSpec((1, GLA_DV), lambda bi, t: (0, 0))],
        out_specs=spec(v.shape[2]),
        out_shape=jax.ShapeDtypeStruct(v.shape, F32),
        scratch_shapes=[pltpu.VMEM((GLA_HEADS, GLA_DV, GLA_DK), F32)],
        compiler_params=_params("parallel", "arbitrary"),
        name="gla",
    )(qk, v, r, la, gn.reshape(1, GLA_DV))


def _cmul(ar, ai, br, bi):
    return ar * br - ai * bi, ar * bi + ai * br


def _s5_prep_body(lr_ref, li_ref, ldt_ref, br_ref, bi_ref, bbr_ref, bbi_ref, pr_ref, pi_ref):
    lr = jnp.minimum(lr_ref[...], -1e-4)
    li = li_ref[...]
    dt = jnp.exp(ldt_ref[...])
    mag = jnp.exp(lr * dt)
    ar = mag * jnp.cos(li * dt)
    ai = mag * jnp.sin(li * dt)
    den = lr * lr + li * li
    nr = ar - 1.0
    fr = (nr * lr + ai * li) / den
    fi = (ai * lr - nr * li) / den
    w = br_ref.shape[2]
    for j in range(br_ref.shape[0]):
        sl = slice(j * w, (j + 1) * w)
        b_re, b_im = br_ref[j], bi_ref[j]
        bbr_ref[j] = (fr[:, sl] * b_re - fi[:, sl] * b_im).astype(BF16)
        bbi_ref[j] = (fr[:, sl] * b_im + fi[:, sl] * b_re).astype(BF16)
    pr_ref[0:1, :] = ar
    pi_ref[0:1, :] = ai
    n = 1
    while n < pr_ref.shape[0]:
        sr, si = pr_ref[n - 1:n, :], pi_ref[n - 1:n, :]
        nr_, ni_ = _cmul(pr_ref[0:n, :], pi_ref[0:n, :], sr, si)
        pr_ref[n:2 * n, :] = nr_
        pi_ref[n:2 * n, :] = ni_
        n *= 2


def _s5_body(u_ref, bbr_ref, bbi_ref, pr_ref, pi_ref, cr_ref, ci_ref, d_ref, wg_ref, bg_ref,
             o_ref, sr_ref, si_ref):
    t = u_ref.shape[0]
    w = bbr_ref.shape[2]
    cw = bbr_ref.shape[1]

    @pl.when(pl.program_id(1) == 0)
    def _():
        sr_ref[...] = jnp.zeros_like(sr_ref)
        si_ref[...] = jnp.zeros_like(si_ref)

    u = u_ref[...]
    ub = u.astype(BF16)
    row = lax.broadcasted_iota(I32, (t, w), 0)
    ys = []
    for j in range(bbr_ref.shape[0]):
        ls = slice(j * w, (j + 1) * w)
        uj = ub[:, j * cw:(j + 1) * cw]
        xr = _dot(uj, bbr_ref[j])
        xi = _dot(uj, bbi_ref[j])
        d = 1
        while d < t:
            ar, ai = pr_ref[d - 1:d, ls], pi_ref[d - 1:d, ls]
            if d < 8:
                shr = jnp.where(row >= d, pltpu.roll(xr, d, axis=0), 0.0)
                shi = jnp.where(row >= d, pltpu.roll(xi, d, axis=0), 0.0)
            else:
                z = jnp.zeros((d, w), F32)
                shr = jnp.concatenate([z, xr[:t - d]], axis=0)
                shi = jnp.concatenate([z, xi[:t - d]], axis=0)
            dr, di = _cmul(ar, ai, shr, shi)
            xr, xi = xr + dr, xi + di
            d *= 2
        dr, di = _cmul(pr_ref[:, ls], pi_ref[:, ls], sr_ref[:, ls], si_ref[:, ls])
        xr, xi = xr + dr, xi + di
        sr_ref[:, ls] = xr[t - 1:t]
        si_ref[:, ls] = xi[t - 1:t]
        ys.append(_dot(xr.astype(BF16), cr_ref[j]) - _dot(xi.astype(BF16), ci_ref[j]))
    y = jnp.concatenate(ys, axis=1) + d_ref[...] * u
    y = jax.nn.gelu(y)
    gate = jax.nn.sigmoid(_dot(y.astype(BF16), wg_ref[...]) + bg_ref[...])
    o_ref[...] = y * gate


def _block_diag(m, split):
    g, r, c = m.shape
    gs = g // split
    eye = jnp.eye(gs, dtype=m.dtype)
    m = m.reshape(split, gs, r, c)
    out = m[:, :, :, None, :] * eye[None, :, None, :, None]
    return out.reshape(split, gs * r, gs * c)


def _s5(u, lam_re, lam_im, log_dt, b_re, b_im, c_re, c_im, dd, w_glu, b_glu):
    b, l, width = u.shape
    g, p = lam_re.shape
    n_state = g * p
    t = S5_TILE
    br_bd = _block_diag(jnp.swapaxes(b_re, 1, 2), S5_SPLIT)
    bi_bd = _block_diag(jnp.swapaxes(b_im, 1, 2), S5_SPLIT)
    cr_bd = _block_diag(jnp.swapaxes(c_re, 1, 2), S5_SPLIT).astype(BF16)
    ci_bd = _block_diag(jnp.swapaxes(c_im, 1, 2), S5_SPLIT).astype(BF16)
    ldt = jnp.broadcast_to(log_dt[:, None], (g, p)).reshape(1, n_state)
    bbr, bbi, pw_r, pw_i = pl.pallas_call(
        _s5_prep_body,
        out_shape=[jax.ShapeDtypeStruct(br_bd.shape, BF16), jax.ShapeDtypeStruct(br_bd.shape, BF16),
                   jax.ShapeDtypeStruct((t, n_state), F32), jax.ShapeDtypeStruct((t, n_state), F32)],
        compiler_params=pltpu.CompilerParams(vmem_limit_bytes=VMEM_LIMIT),
        name="s5_prep",
    )(lam_re.reshape(1, n_state), lam_im.reshape(1, n_state), ldt, br_bd, bi_bd)

    consts = [bbr, bbi, pw_r, pw_i, cr_bd, ci_bd, dd.reshape(1, width), w_glu.astype(BF16),
              b_glu.reshape(1, width)]
    return pl.pallas_call(
        _s5_body,
        grid=(b, l // t),
        in_specs=[pl.BlockSpec((None, t, width), lambda bi_, ti: (bi_, ti, 0))]
        + [_const_spec(c) for c in consts],
        out_specs=pl.BlockSpec((None, t, width), lambda bi_, ti: (bi_, ti, 0)),
        out_shape=jax.ShapeDtypeStruct(u.shape, F32),
        scratch_shapes=[pltpu.VMEM((1, n_state), F32), pltpu.VMEM((1, n_state), F32)],
        compiler_params=_params("parallel", "arbitrary"),
        name="s5",
    )(u, *consts)


def _rope(x, cos, sin_signed, half):
    lane = lax.broadcasted_iota(I32, x.shape, 1) % (2 * half)
    rot = jnp.where(lane < half, pltpu.roll(x, LANES - half, axis=1), pltpu.roll(x, half, axis=1))
    return x * cos + rot * sin_signed


def _od_in_body(h_ref, ca_ref, sa_ref, ci_ref, si_ref, g_ref, wm_ref, wk_ref, ww_ref,
                q_ref, k_ref, v_ref, qi_ref, ki_ref, wi_ref):
    n = _rms(h_ref[...], g_ref[...]).astype(BF16)
    main = _dot(n, wm_ref[...])
    ca, sa, ci, si = ca_ref[...], sa_ref[...], ci_ref[...], si_ref[...]
    hd = ATT_HEAD_DIM
    nq, nk = q_ref.shape[0] * hd, k_ref.shape[1]
    for h in range(nq // hd):
        sl = slice(h * hd, (h + 1) * hd)
        q_ref[h] = (_rope(main[:, sl], ca, sa, hd // 2) * (hd ** -0.5 * math.log2(math.e))).astype(BF16)
    for h in range(nk // hd):
        k_ref[:, h * hd:(h + 1) * hd] = _rope(
            main[:, nq + h * hd:nq + (h + 1) * hd], ca, sa, hd // 2).astype(BF16)
    v_ref[...] = main[:, nq + nk:nq + 2 * nk].astype(BF16)
    o = nq + 2 * nk
    first = lax.broadcasted_iota(I32, (h_ref.shape[0], LANES), 1) < IDX_DIM
    for s in range(IDX_HEADS * IDX_DIM // LANES):
        pair = _rope(main[:, o + s * LANES:o + (s + 1) * LANES], ci, si, IDX_DIM // 2)
        qi_ref[:, 2 * s * LANES:(2 * s + 1) * LANES] = jnp.where(first, pair, 0.0).astype(BF16)
        qi_ref[:, (2 * s + 1) * LANES:(2 * s + 2) * LANES] = jnp.where(
            first, pltpu.roll(pair, IDX_DIM, axis=1), 0.0).astype(BF16)
    ki_ref[...] = _rope(_dot(n, wk_ref[...]), ci, si, IDX_DIM // 2).astype(BF16)
    wi_ref[...] = _dot(n, ww_ref[...]) * (IDX_HEADS ** -0.5 * IDX_DIM ** -0.5)


def _rope_tables(positions, dim):
    inv = ROPE_THETA ** (-jnp.arange(0, dim, 2, dtype=F32) / dim)
    ang = positions.astype(F32)[..., None] * inv
    cos = jnp.cos(ang)
    sin = jnp.sin(ang)
    reps = LANES // dim
    cos = jnp.tile(jnp.concatenate([cos, cos], -1), (1, 1, reps))
    sin = jnp.tile(jnp.concatenate([-sin, sin], -1), (1, 1, reps))
    return cos.reshape(-1, LANES), sin.reshape(-1, LANES)


def _od_in(h, g, w_in, tables):
    d = h.shape[1]
    nq = ATT_HEADS * ATT_HEAD_DIM
    nk = ATT_KV_HEADS * ATT_HEAD_DIM
    nqi = IDX_HEADS * IDX_DIM
    n_main = nq + 2 * nk + nqi
    w_main = w_in[:, :n_main]
    w_ki = jnp.pad(w_in[:, n_main:n_main + IDX_DIM], ((0, 0), (0, LANES - IDX_DIM)))
    w_wi = jnp.pad(w_in[:, n_main + IDX_DIM:], ((0, 0), (0, LANES - IDX_HEADS)))
    consts = [g.reshape(1, d), w_main.astype(BF16), w_ki.astype(BF16), w_wi.astype(BF16)]
    outs = [(ATT_HEADS, ATT_HEAD_DIM, BF16), (nk, BF16), (nk, BF16), (IDX_HEADS * LANES, BF16),
            (LANES, BF16), (LANES, F32)]
    return _row_tiled(_od_in_body, [h] + list(tables), consts, outs, ROW_TILE, "od_in")


def _sort_key(x):
    bits = lax.bitcast_convert_type(x, I32)
    return bits ^ ((bits >> 31) & jnp.int32(0x7FFFFFFF))


def _key_value(k):
    return lax.bitcast_convert_type(k ^ ((k >> 31) & jnp.int32(0x7FFFFFFF)), F32)


def _slab_reduce(x, op):
    ways = 4
    parts = [x[r * 8:(r + 1) * 8] for r in range(ways)]
    for r in range(ways, x.shape[0] // 8):
        parts[r % ways] = op(parts[r % ways], x[r * 8:(r + 1) * 8])
    return op(op(parts[0], parts[1]), op(parts[2], parts[3]))


def _sublane_all(x, op):
    for s in (4, 2, 1):
        x = op(x, pltpu.roll(x, s, axis=0))
    return x


def _dsa_body(qi_ref, wi_ref, q_ref, ki_ref, k_ref, vt_ref, o_ref,
              sc_ref, bias_ref, m_ref, acc_ref, *, topk):
    tq = q_ref.shape[1]
    tk = sc_ref.shape[1]
    hd = ATT_HEAD_DIM
    group = ATT_HEADS // ATT_KV_HEADS
    r0 = pl.program_id(1) * tq
    nkb = (r0 + tq + tk - 1) // tk
    qpos = r0 + lax.broadcasted_iota(I32, (1, tq), 1)
    key_ninf = jnp.int32(-2139095041)
    key_lo0 = key_ninf + 1

    def kpos_of(kb):
        return kb * tk + lax.broadcasted_iota(I32, (tk, 1), 0)

    wit = wi_ref[...].T

    def score_block(kb, ext):
        kmax, kmin = ext
        c0 = pl.multiple_of(kb * tk, tk)
        kib = ki_ref[pl.ds(c0, tk), :]
        acc = jnp.zeros((tk, tq), F32)
        for h in range(IDX_HEADS):
            lg = lax.dot_general(kib, qi_ref[:, h * LANES:(h + 1) * LANES], _NT,
                                 preferred_element_type=F32)
            acc = acc + jnp.maximum(lg, 0.0) * wit[h:h + 1, :]
        acc = jnp.where(acc == 0.0, 0.0, acc)
        valid = kpos_of(kb) <= qpos
        key = _sort_key(acc)
        keys = jnp.where(valid, key, key_ninf)
        sc_ref[kb] = keys
        return (jnp.maximum(kmax, _slab_reduce(keys, jnp.maximum)),
                jnp.minimum(kmin, _slab_reduce(jnp.where(valid, key, int_max), jnp.minimum)))

    int_max = jnp.int32(2147483647)
    kmax, kmin = lax.fori_loop(0, nkb, score_block, (jnp.full((8, tq), key_ninf, I32),
                                                    jnp.full((8, tq), int_max, I32)))
    kmax = _sublane_all(kmax, jnp.maximum)[0:1]
    kmin = _sublane_all(kmin, jnp.minimum)[0:1]

    def count(pred):
        def body(kb, cnt):
            return cnt + _slab_reduce(jnp.where(pred(sc_ref[kb], kb), 1.0, 0.0), jnp.add)
        cnt = lax.fori_loop(0, nkb, body, jnp.zeros((8, tq), F32))
        return _sublane_all(cnt, jnp.add)[0:1]

    kf = jnp.float32(topk)
    short = (qpos + 1) <= topk

    def search_cond(st):
        return jnp.logical_and(st[0] < 200, st[-1] > 0.0)

    def search_body(st):
        it, lo, hi, c_lo, c_hi, n_acc, side, streak, _ = st
        mixed = jnp.logical_and(lo < 0, hi > 0)
        w = hi - lo
        active = jnp.logical_not(jnp.logical_or(
            short, jnp.logical_or(c_lo == kf, jnp.logical_and(jnp.logical_not(mixed), w <= 1))))
        gal = lax.shift_left(jnp.int32(1 << 22), jnp.clip(it - 1, 0, 8))
        frac = (c_lo - kf - 0.5) / (c_lo - c_hi)
        d_key = (w.astype(F32) * frac).astype(I32)
        v_lo, v_hi = _key_value(lo), _key_value(hi)
        d_val = _sort_key(v_lo + (v_hi - v_lo) * frac) - lo
        neg = hi <= 0
        d = jnp.where(streak >= 2, w >> 1, jnp.where(neg, d_val, d_key))
        d = jnp.where(jnp.logical_and(n_acc == 0, jnp.logical_not(neg)),
                      w - jnp.minimum(gal, w - 1), d)
        cand = jnp.where(mixed, 0, lo + jnp.clip(d, 1, jnp.maximum(w - 1, 1)))
        cnt = count(lambda keys, kb: keys >= cand)
        up = jnp.logical_and(active, cnt >= kf)
        dn = jnp.logical_and(active, cnt < kf)
        lo = jnp.where(up, cand, lo)
        c_lo = jnp.where(up, cnt, c_lo)
        hi = jnp.where(dn, cand, hi)
        c_hi = jnp.where(dn, cnt, c_hi)
        n_acc = n_acc + jnp.where(jnp.logical_and(up, jnp.logical_not(mixed)), 1, 0)
        new_side = jnp.where(up, 1, -1)
        streak = jnp.where(new_side == side, streak + 1, 1)
        mixed = jnp.logical_and(lo < 0, hi > 0)
        settled = jnp.logical_or(
            short, jnp.logical_or(c_lo == kf, jnp.logical_and(jnp.logical_not(mixed), hi - lo <= 1)))
        pending = jnp.sum(jnp.where(settled, 0.0, 1.0))
        return it + 1, lo, hi, c_lo, c_hi, n_acc, new_side, streak, pending

    zi = jnp.zeros((1, tq), I32)
    st0 = (jnp.int32(0), kmin, kmax + 1, (qpos + 1).astype(F32), jnp.zeros((1, tq), F32),
           zi, zi, zi, jnp.float32(1.0))
    st = lax.while_loop(search_cond, search_body, st0)
    thr = jnp.where(short, key_lo0, st[1])
    cnt_thr = st[3]

    tied = jnp.logical_and(cnt_thr > kf, jnp.logical_not(short))
    n_tied = jnp.sum(jnp.where(tied, 1.0, 0.0))

    @pl.when(n_tied > 0.0)
    def _():
        need = kf - st[4]
        ltri = jnp.where(lax.broadcasted_iota(I32, (tk, tk), 0) >= lax.broadcasted_iota(I32, (tk, tk), 1),
                         1.0, 0.0).astype(BF16)

        def demote(kb, before):
            keys = sc_ref[kb]
            is_tie = keys == thr
            rank = _dot(ltri, jnp.where(is_tie, 1.0, 0.0).astype(BF16)) + before
            drop = jnp.logical_and(jnp.logical_and(is_tie, rank > need), tied)
            sc_ref[kb] = jnp.where(drop, key_ninf, keys)
            return rank[tk - 1:tk, :]

        lax.fori_loop(0, nkb, demote, jnp.zeros((1, tq), F32))

    def to_bias(kb, carry):
        bias_ref[kb] = jnp.where(sc_ref[kb] >= thr, 0.0, NEG).astype(BF16)
        return carry

    lax.fori_loop(0, nkb, to_bias, 0)

    m_ref[...] = jnp.full_like(m_ref, NEG)
    acc_ref[...] = jnp.zeros_like(acc_ref)
    va = vt_ref.shape[1] // ATT_KV_HEADS

    def attend(kb, carry):
        c0 = pl.multiple_of(kb * tk, tk)
        bias = jnp.concatenate([bias_ref[kb]] * group, axis=1)
        for kv in range(ATT_KV_HEADS):
            kblk = k_ref[pl.ds(c0, tk), kv * hd:(kv + 1) * hd]
            qg = q_ref[kv * group:(kv + 1) * group].reshape(group * tq, hd)
            s = lax.dot_general(kblk, qg, _NT, preferred_element_type=F32).astype(BF16) + bias
            m_old = m_ref[kv:kv + 1, :]
            m_new = jnp.maximum(m_old, jnp.max(s, axis=0, keepdims=True).astype(F32))
            p = jnp.exp2(s - m_new.astype(BF16))
            acc_ref[kv] = jnp.exp2(m_old - m_new) * acc_ref[kv] + _dot(
                vt_ref[kb, kv * va:(kv + 1) * va, :], p)
            m_ref[kv:kv + 1, :] = m_new
        return carry

    lax.fori_loop(0, nkb, attend, 0)
    for h in range(ATT_HEADS):
        kv, cs = h // group, slice((h % group) * tq, (h % group + 1) * tq)
        o_ref[:, h * hd:(h + 1) * hd] = (
            acc_ref[kv, 0:hd, cs] / acc_ref[kv, hd:hd + 1, cs]).T.astype(o_ref.dtype)


def _dsa(q, k, v, qi, ki, wi):
    b, l, nv = v.shape
    nh, _, hd = q.shape
    tq, tk = min(DSA_TQ, l), min(DSA_TK, l)
    topk = min(TOPK_MAX, l // 4)
    group = nh // ATT_KV_HEADS
    vt = jnp.swapaxes(v.reshape(b, l // tk, tk, ATT_KV_HEADS, hd), 2, 4)
    vt = vt.swapaxes(2, 3)
    extra = jnp.zeros((b, l // tk, ATT_KV_HEADS, SUBLANES_BF16, tk), v.dtype).at[:, :, :, 0].set(1.0)
    vt = jnp.concatenate([vt, extra], axis=3)
    va = hd + SUBLANES_BF16
    vt = vt.reshape(b, l // tk, ATT_KV_HEADS * va, tk)

    def qspec(w):
        return pl.BlockSpec((None, tq, w), lambda bi, i: (bi, i, 0))

    def kspec(w):
        return pl.BlockSpec((None, l, w), lambda bi, i: (bi, 0, 0))

    return pl.pallas_call(
        functools.partial(_dsa_body, topk=topk),
        grid=(b, l // tq),
        in_specs=[qspec(qi.shape[2]), qspec(wi.shape[2]),
                  pl.BlockSpec((nh, tq, hd), lambda bi, i: (0, bi * (l // tq) + i, 0)),
                  kspec(ki.shape[2]), kspec(k.shape[2]),
                  pl.BlockSpec((None, l // tk, ATT_KV_HEADS * va, tk), lambda bi, i: (bi, 0, 0, 0))],
        out_specs=qspec(nh * hd),
        out_shape=jax.ShapeDtypeStruct((b, l, nh * hd), BF16),
        scratch_shapes=[pltpu.VMEM((l // tk, tk, tq), I32),
                        pltpu.VMEM((l // tk, tk, tq), BF16),
                        pltpu.VMEM((ATT_KV_HEADS, group * tq), F32),
                        pltpu.VMEM((ATT_KV_HEADS, va, group * tq), F32)],
        compiler_params=_params("parallel", "arbitrary"),
        name="dsa",
    )(qi, wi, q, ki, k, vt)


def _xattn_body(h_ref, g_ref, wq_ref, kv_ref, wo_ref, o_ref):
    x = h_ref[...]
    d = x.shape[1]
    hd = d // XA_HEADS
    n = _rms(x, g_ref[...]).astype(BF16)
    q = (_dot(n, wq_ref[...].astype(BF16)) * (hd ** -0.5)).astype(BF16)
    outs = []
    for h in range(XA_HEADS):
        hs = slice(h * hd, (h + 1) * hd)
        s = lax.dot_general(q[:, hs], kv_ref[:, hs], _NT, preferred_element_type=F32)
        p = jnp.exp(s - jnp.max(s, axis=1, keepdims=True))
        p = p / jnp.sum(p, axis=1, keepdims=True)
        outs.append(_dot(p.astype(BF16), kv_ref[:, d + h * hd:d + (h + 1) * hd]))
    o = jnp.concatenate(outs, axis=1).astype(BF16)
    o_ref[...] = x + _dot(o, wo_ref[...].astype(BF16))


def _xattn(h, g, wq, kv, wo, layer, rows_per_batch):
    m, d = h.shape
    tm = ROW_TILE
    per = rows_per_batch // tm
    return pl.pallas_call(
        _xattn_body,
        grid=(m // tm,),
        in_specs=[pl.BlockSpec((tm, d), lambda i: (i, 0)), _const_spec(g.reshape(1, d)),
                  _const_spec((wq, layer)),
                  pl.BlockSpec((None,) + kv.shape[1:], lambda i: (i // per, 0, 0)),
                  _const_spec((wo, layer))],
        out_specs=pl.BlockSpec((tm, d), lambda i: (i, 0)),
        out_shape=jax.ShapeDtypeStruct((m, d), F32),
        compiler_params=_params("parallel"),
        name="xattn",
    )(h, g.reshape(1, d), wq, kv, wo)


def kernel(x, mem, positions, ffn1_norm, ffn1_w_gate, ffn1_w_up, ffn1_w_down, mix_norm, ev_w_in, ev_w_out, gla_w_alpha, gla_b_alpha, gla_norm, s5_lambda_re, s5_lambda_im, s5_log_dt, s5_B_re, s5_B_im, s5_C_re, s5_C_im, s5_D, s5_w_glu, s5_b_glu, od_w_in, od_w_out, xa_norm, mem_norm, xa_wq, xa_wk, xa_wv, xa_wo, ffn2_norm, ffn2_w_gate, ffn2_w_up, ffn2_w_down, final_norm):
    b, l, d = x.shape
    n_mem = mem.shape[1]
    depth = ffn1_norm.shape[0]
    tables = _rope_tables(positions, ATT_HEAD_DIM) + _rope_tables(positions, IDX_DIM)
    mem2 = mem.reshape(b * n_mem, d)
    h = x.reshape(b * l, d)

    def seq(a):
        return a.reshape(b, l, a.shape[-1])

    def flat(a):
        return a.reshape(b * l, a.shape[-1])

    for layer in range(depth):
        h = _ffn(h, ffn1_norm[layer], ffn1_w_gate, ffn1_w_up, ffn1_w_down, layer)
        if layer % 2 == 0:
            e = layer // 2
            qk, v, r, la, u = _ev_in(h, mix_norm[layer], ev_w_in[e], gla_w_alpha[e], gla_b_alpha[e])
            o_gla = _gla(seq(qk), seq(v), seq(r), seq(la), gla_norm[e])
            o_s5 = _s5(seq(u), s5_lambda_re[e], s5_lambda_im[e], s5_log_dt[e], s5_B_re[e],
                       s5_B_im[e], s5_C_re[e], s5_C_im[e], s5_D[e], s5_w_glu[e], s5_b_glu[e])
            h = _proj_res(h, [flat(o_gla), flat(o_s5)], ev_w_out, e)
        else:
            o = layer // 2
            q, k, v, qi, ki, wi = _od_in(h, mix_norm[layer], od_w_in[o], tables)
            att = _dsa(q, seq(k), seq(v), seq(qi), seq(ki), seq(wi))
            h = _proj_res(h, [flat(att)], od_w_out, o)
        kv = _norm_proj(mem2, mem_norm, jnp.concatenate([xa_wk[layer], xa_wv[layer]], axis=1),
                        n_mem, BF16)
        h = _xattn(h, xa_norm[layer], xa_wq, kv.reshape(b, n_mem, 2 * d), xa_wo, layer, l)
        last = layer == depth - 1
        h = _ffn(h, ffn2_norm[layer], ffn2_w_gate, ffn2_w_up, ffn2_w_down, layer,
                 final_g=final_norm if last else None)
    return h.reshape(b, l, d)
```

```python
import functools
import math

import jax
import jax.numpy as jnp
from jax import lax
from jax.experimental import pallas as pl
from jax.experimental.pallas import tpu as pltpu

F32 = jnp.float32
BF16 = jnp.bfloat16
I32 = jnp.int32

EPS = 1e-6
ROPE_THETA = 10000.0
GLA_HEADS, GLA_DK, GLA_DV, GLA_RANK, GLA_TAU, GLA_CHUNK = 4, 64, 128, 16, 16.0, 64
S5_GROUP, S5_STATE = 16, 64
ATT_HEADS, ATT_KV_HEADS, ATT_HEAD_DIM = 8, 2, 128
IDX_HEADS, IDX_DIM = 8, 64
TOPK_MAX = 256
XA_HEADS = 4

LANES = 128
VMEM_LIMIT = 56 * 1024 * 1024
ROW_TILE = 512
FFN_CHUNK = 256
GLA_TILE = 256
S5_TILE = 256
S5_BLOCK = 8
S5_SPLIT = 4
DSA_TQ = 256
DSA_TK = 512
NEG = -(2.0 ** 100)
SUBLANES_BF16 = 16

_NT = (((1,), (1,)), ((), ()))
_TN = (((0,), (0,)), ((), ()))


def _params(*sem):
    return pltpu.CompilerParams(dimension_semantics=sem, vmem_limit_bytes=VMEM_LIMIT)


def _const_spec(a):
    if isinstance(a, tuple):
        a, layer = a
        nd = a.ndim - 1
        return pl.BlockSpec((None,) + a.shape[1:], lambda *_: (layer,) + (0,) * nd,
                            pipeline_mode=pl.Buffered(1))
    nd = a.ndim
    return pl.BlockSpec(a.shape, lambda *_: (0,) * nd, pipeline_mode=pl.Buffered(1))


def _rms(x, g):
    return x * lax.rsqrt(jnp.mean(x * x, axis=-1, keepdims=True) + EPS) * g


def _dot(a, b):
    return jnp.dot(a, b, preferred_element_type=F32)


def _row_tiled(body, rows, consts, outs, tm, name):
    m = rows[0].shape[0]
    assert m % tm == 0

    def out_desc(o):
        if len(o) == 2:
            return pl.BlockSpec((tm, o[0]), lambda i: (i, 0)), jax.ShapeDtypeStruct((m, o[0]), o[1])
        return (pl.BlockSpec((o[0], tm, o[1]), lambda i: (0, i, 0)),
                jax.ShapeDtypeStruct((o[0], m, o[1]), o[2]))

    descs = [out_desc(o) for o in outs]
    return pl.pallas_call(
        body,
        grid=(m // tm,),
        in_specs=[pl.BlockSpec((tm, r.shape[1]), lambda i: (i, 0)) for r in rows]
        + [_const_spec(c) for c in consts],
        out_specs=[d[0] for d in descs],
        out_shape=[d[1] for d in descs],
        compiler_params=_params("parallel"),
        name=name,
    )(*rows, *[c[0] if isinstance(c, tuple) else c for c in consts])


def _ffn_body(h_ref, g_ref, wg_ref, wu_ref, wd_ref, *rest, final):
    o_ref = rest[-1]
    x = h_ref[...]
    n = _rms(x, g_ref[...]).astype(BF16)
    acc = jnp.zeros_like(x)
    for c in range(wg_ref.shape[1] // FFN_CHUNK):
        sl = slice(c * FFN_CHUNK, (c + 1) * FFN_CHUNK)
        g = _dot(n, wg_ref[:, sl].astype(BF16))
        u = _dot(n, wu_ref[:, sl].astype(BF16))
        a = (g * jax.nn.sigmoid(g) * u).astype(BF16)
        acc = acc + _dot(a, wd_ref[sl, :].astype(BF16))
    y = x + 0.5 * acc
    if final:
        y = _rms(y, rest[0][...])
    o_ref[...] = y


def _ffn(h, g, wg, wu, wd, layer, final_g=None):
    d = h.shape[1]
    consts = [g.reshape(1, d), (wg, layer), (wu, layer), (wd, layer)]
    if final_g is not None:
        consts.append(final_g.reshape(1, d))
    body = functools.partial(_ffn_body, final=final_g is not None)
    return _row_tiled(body, [h], consts, [(d, F32)], ROW_TILE, "ffn")[0]


def _proj_res_body(*refs):
    h_ref, w_ref, o_ref = refs[0], refs[-2], refs[-1]
    acc = h_ref[...]
    r0 = 0
    for x_ref in refs[1:-2]:
        k = x_ref.shape[1]
        acc = acc + _dot(x_ref[...].astype(BF16), w_ref[r0:r0 + k, :].astype(BF16))
        r0 += k
    o_ref[...] = acc


def _proj_res(h, xs, w_stack, layer):
    return _row_tiled(_proj_res_body, [h] + list(xs), [(w_stack, layer)],
                      [(h.shape[1], F32)], ROW_TILE, "proj_res")[0]


def _norm_proj_body(x_ref, g_ref, w_ref, o_ref):
    n = _rms(x_ref[...], g_ref[...]).astype(BF16)
    o_ref[...] = _dot(n, w_ref[...]).astype(o_ref.dtype)


def _norm_proj(x, g, w, tm, out_dtype):
    return _row_tiled(_norm_proj_body, [x], [g.reshape(1, -1), w.astype(BF16)],
                      [(w.shape[1], out_dtype)], tm, "norm_proj")[0]


def _log_sigmoid(x):
    return jnp.minimum(x, 0.0) - jnp.log(1.0 + jnp.exp(-jnp.abs(x)))


def _ev_in_body(h_ref, g_ref, wm_ref, wa_ref, wa2_ref, ba_ref, wu_ref,
                qk_ref, v_ref, r_ref, la_ref, u_ref):
    n = _rms(h_ref[...], g_ref[...]).astype(BF16)
    nqk = qk_ref.shape[1]
    nv = v_ref.shape[1]
    main = _dot(n, wm_ref[...])
    qk_ref[...] = main[:, :nqk]
    v_ref[...] = main[:, nqk:nqk + nv]
    r_ref[...] = main[:, nqk + nv:]
    a_low = _dot(n, wa_ref[...]).astype(BF16)
    alpha = _dot(a_low, wa2_ref[...]) + ba_ref[...]
    la_ref[...] = _log_sigmoid(alpha) / GLA_TAU
    u_ref[...] = _dot(n, wu_ref[...])


def _ev_in(h, g, w_in, w_a2, b_a):
    d = h.shape[1]
    nqk = 2 * GLA_HEADS * GLA_DK
    nv = GLA_HEADS * GLA_DV
    n_main = nqk + 2 * nv
    w_main = w_in[:, :n_main]
    w_a = jnp.pad(w_in[:, n_main:n_main + GLA_RANK], ((0, 0), (0, LANES - GLA_RANK)))
    w_u = w_in[:, n_main + GLA_RANK:]
    w_a2p = jnp.pad(w_a2, ((0, LANES - GLA_RANK), (0, 0)))
    consts = [g.reshape(1, d), w_main.astype(BF16), w_a.astype(BF16), w_a2p.astype(BF16),
              b_a.reshape(1, -1), w_u.astype(BF16)]
    outs = [(nqk, F32), (nv, F32), (nv, F32), (GLA_HEADS * GLA_DK, F32), (w_u.shape[1], F32)]
    return _row_tiled(_ev_in_body, [h], consts, outs, ROW_TILE, "ev_in")


def _gla_body(qk_ref, v_ref, r_ref, la_ref, gn_ref, o_ref, st_ref):
    tl = qk_ref.shape[0]
    hdk = GLA_HEADS * GLA_DK
    c = GLA_CHUNK

    @pl.when(pl.program_id(1) == 0)
    def _():
        st_ref[...] = jnp.zeros_like(st_ref)

    row = lax.broadcasted_iota(I32, (tl, hdk), 0) % c
    bc = la_ref[...]
    d = 1
    while d < c:
        bc = bc + jnp.where(row >= d, pltpu.roll(bc, d, axis=0), 0.0)
        d *= 2
    q_dec = qk_ref[:, :hdk] * (GLA_DK ** -0.5) * jnp.exp(bc)
    k = qk_ref[:, hdk:]
    k_inv = k * jnp.exp(-bc)
    causal = lax.broadcasted_iota(I32, (c, c), 0) >= lax.broadcasted_iota(I32, (c, c), 1)
    gn = gn_ref[...]
    for j in range(tl // c):
        rs = slice(j * c, (j + 1) * c)
        b_last = bc[j * c + c - 1:j * c + c, :]
        k_end = k[rs] * jnp.exp(b_last - bc[rs])
        dec = jnp.exp(b_last)
        for h in range(GLA_HEADS):
            ks = slice(h * GLA_DK, (h + 1) * GLA_DK)
            vs = slice(h * GLA_DV, (h + 1) * GLA_DV)
            qd = q_dec[rs, ks].astype(BF16)
            vh = v_ref[rs, vs].astype(BF16)
            s = lax.dot_general(qd, k_inv[rs, ks].astype(BF16), _NT, preferred_element_type=F32)
            s = jnp.where(causal, s, 0.0).astype(BF16)
            st = st_ref[h]
            o = _dot(s, vh) + lax.dot_general(qd, st.astype(BF16), _NT, preferred_element_type=F32)
            st_ref[h] = st * dec[:, ks] + lax.dot_general(
                vh, k_end[:, ks].astype(BF16), _TN, preferred_element_type=F32)
            o = o * lax.rsqrt(jnp.mean(o * o, axis=-1, keepdims=True) + EPS) * gn
            r = r_ref[rs, vs]
            o_ref[rs, vs] = o * (r * jax.nn.sigmoid(r))


def _gla(qk, v, r, la, gn):
    b, l, _ = qk.shape
    tl = GLA_TILE

    def spec(w):
        return pl.BlockSpec((None, tl, w), lambda bi, t: (bi, t, 0))

    return pl.pallas_call(
        _gla_body,
        grid=(b, l // tl),
        in_specs=[spec(qk.shape[2]), spec(v.shape[2]), spec(r.shape[2]), spec(la.shape[2]),
                  pl.BlockSpec((1, GLA_DV), lambda bi, t: (0, 0))],
        out_specs=spec(v.shape[2]),
        out_shape=jax.ShapeDtypeStruct(v.shape, F32),
        scratch_shapes=[pltpu.VMEM((GLA_HEADS, GLA_DV, GLA_DK), F32)],
        compiler_params=_params("parallel", "arbitrary"),
        name="gla",
    )(qk, v, r, la, gn.reshape(1, GLA_DV))


def _cmul(ar, ai, br, bi):
    return ar * br - ai * bi, ar * bi + ai * br


def _s5_prep_body(lr_ref, li_ref, ldt_ref, br_ref, bi_ref, bbr_ref, bbi_ref, pr_ref, pi_ref):
    lr = jnp.minimum(lr_ref[...], -1e-4)
    li = li_ref[...]
    dt = jnp.exp(ldt_ref[...])
    mag = jnp.exp(lr * dt)
    ar = mag * jnp.cos(li * dt)
    ai = mag * jnp.sin(li * dt)
    den = lr * lr + li * li
    nr = ar - 1.0
    fr = (nr * lr + ai * li) / den
    fi = (ai * lr - nr * li) / den
    w = br_ref.shape[2]
    for j in range(br_ref.shape[0]):
        sl = slice(j * w, (j + 1) * w)
        b_re, b_im = br_ref[j], bi_ref[j]
        bbr_ref[j] = (fr[:, sl] * b_re - fi[:, sl] * b_im).astype(BF16)
        bbi_ref[j] = (fr[:, sl] * b_im + fi[:, sl] * b_re).astype(BF16)
    pr_ref[0:1, :] = ar
    pi_ref[0:1, :] = ai
    n = 1
    while n < pr_ref.shape[0]:
        sr, si = pr_ref[n - 1:n, :], pi_ref[n - 1:n, :]
        nr_, ni_ = _cmul(pr_ref[0:n, :], pi_ref[0:n, :], sr, si)
        pr_ref[n:2 * n, :] = nr_
        pi_ref[n:2 * n, :] = ni_
        n *= 2


def _s5_body(u_ref, bbr_ref, bbi_ref, pr_ref, pi_ref, cr_ref, ci_ref, d_ref, wg_ref, bg_ref,
             o_ref, sr_ref, si_ref):
    t = u_ref.shape[0]
    w = bbr_ref.shape[2]
    cw = bbr_ref.shape[1]

    @pl.when(pl.program_id(1) == 0)
    def _():
        sr_ref[...] = jnp.zeros_like(sr_ref)
        si_ref[...] = jnp.zeros_like(si_ref)

    u = u_ref[...]
    ub = u.astype(BF16)
    nb = pr_ref.shape[0]
    row = lax.broadcasted_iota(I32, (1, nb, w), 1)
    ys = []
    for j in range(bbr_ref.shape[0]):
        ls = slice(j * w, (j + 1) * w)
        uj = ub[:, j * cw:(j + 1) * cw]
        xr = _dot(uj, bbr_ref[j]).reshape(t // nb, nb, w)
        xi = _dot(uj, bbi_ref[j]).reshape(t // nb, nb, w)
        d = 1
        while d < nb:
            ar = jnp.where(row >= d, pr_ref[d - 1:d, ls], 0.0)
            ai = jnp.where(row >= d, pi_ref[d - 1:d, ls], 0.0)
            dr, di = _cmul(ar, ai, pltpu.roll(xr, d, axis=1), pltpu.roll(xi, d, axis=1))
            xr, xi = xr + dr, xi + di
            d *= 2
        xr, xi = xr.reshape(t, w), xi.reshape(t, w)
        p_r, p_i = pr_ref[:, ls], pi_ref[:, ls]
        c_r, c_i = sr_ref[:, ls], si_ref[:, ls]
        blocks_r, blocks_i = [], []
        for blk in range(t // nb):
            dr, di = _cmul(p_r, p_i, c_r, c_i)
            b_r = xr[blk * nb:(blk + 1) * nb] + dr
            b_i = xi[blk * nb:(blk + 1) * nb] + di
            c_r, c_i = b_r[nb - 1:nb], b_i[nb - 1:nb]
            blocks_r.append(b_r)
            blocks_i.append(b_i)
        sr_ref[:, ls] = c_r
        si_ref[:, ls] = c_i
        xr = jnp.concatenate(blocks_r, axis=0)
        xi = jnp.concatenate(blocks_i, axis=0)
        ys.append(_dot(xr.astype(BF16), cr_ref[j]) - _dot(xi.astype(BF16), ci_ref[j]))
    y = jnp.concatenate(ys, axis=1) + d_ref[...] * u
    y = jax.nn.gelu(y)
    gate = jax.nn.sigmoid(_dot(y.astype(BF16), wg_ref[...]) + bg_ref[...])
    o_ref[...] = y * gate


def _block_diag(m, split):
    g, r, c = m.shape
    gs = g // split
    eye = jnp.eye(gs, dtype=m.dtype)
    m = m.reshape(split, gs, r, c)
    out = m[:, :, :, None, :] * eye[None, :, None, :, None]
    return out.reshape(split, gs * r, gs * c)


def _s5(u, lam_re, lam_im, log_dt, b_re, b_im, c_re, c_im, dd, w_glu, b_glu):
    b, l, width = u.shape
    g, p = lam_re.shape
    n_state = g * p
    t = S5_TILE
    br_bd = _block_diag(jnp.swapaxes(b_re, 1, 2), S5_SPLIT)
    bi_bd = _block_diag(jnp.swapaxes(b_im, 1, 2), S5_SPLIT)
    cr_bd = _block_diag(jnp.swapaxes(c_re, 1, 2), S5_SPLIT).astype(BF16)
    ci_bd = _block_diag(jnp.swapaxes(c_im, 1, 2), S5_SPLIT).astype(BF16)
    ldt = jnp.broadcast_to(log_dt[:, None], (g, p)).reshape(1, n_state)
    bbr, bbi, pw_r, pw_i = pl.pallas_call(
        _s5_prep_body,
        out_shape=[jax.ShapeDtypeStruct(br_bd.shape, BF16), jax.ShapeDtypeStruct(br_bd.shape, BF16),
                   jax.ShapeDtypeStruct((S5_BLOCK, n_state), F32),
                   jax.ShapeDtypeStruct((S5_BLOCK, n_state), F32)],
        compiler_params=pltpu.CompilerParams(vmem_limit_bytes=VMEM_LIMIT),
        name="s5_prep",
    )(lam_re.reshape(1, n_state), lam_im.reshape(1, n_state), ldt, br_bd, bi_bd)

    consts = [bbr, bbi, pw_r, pw_i, cr_bd, ci_bd, dd.reshape(1, width), w_glu.astype(BF16),
              b_glu.reshape(1, width)]
    return pl.pallas_call(
        _s5_body,
        grid=(b, l // t),
        in_specs=[pl.BlockSpec((None, t, width), lambda bi_, ti: (bi_, ti, 0))]
        + [_const_spec(c) for c in consts],
        out_specs=pl.BlockSpec((None, t, width), lambda bi_, ti: (bi_, ti, 0)),
        out_shape=jax.ShapeDtypeStruct(u.shape, F32),
        scratch_shapes=[pltpu.VMEM((1, n_state), F32), pltpu.VMEM((1, n_state), F32)],
        compiler_params=_params("parallel", "arbitrary"),
        name="s5",
    )(u, *consts)


def _rope(x, cos, sin_signed, half):
    lane = lax.broadcasted_iota(I32, x.shape, 1) % (2 * half)
    rot = jnp.where(lane < half, pltpu.roll(x, LANES - half, axis=1), pltpu.roll(x, half, axis=1))
    return x * cos + rot * sin_signed


def _od_in_body(h_ref, ca_ref, sa_ref, ci_ref, si_ref, g_ref, wm_ref, wk_ref, ww_ref,
                q_ref, k_ref, v_ref, qi_ref, ki_ref, wi_ref):
    n = _rms(h_ref[...], g_ref[...]).astype(BF16)
    main = _dot(n, wm_ref[...])
    ca, sa, ci, si = ca_ref[...], sa_ref[...], ci_ref[...], si_ref[...]
    hd = ATT_HEAD_DIM
    nq, nk = q_ref.shape[0] * hd, k_ref.shape[1]
    for h in range(nq // hd):
        sl = slice(h * hd, (h + 1) * hd)
        q_ref[h] = (_rope(main[:, sl], ca, sa, hd // 2) * (hd ** -0.5 * math.log2(math.e))).astype(BF16)
    for h in range(nk // hd):
        k_ref[:, h * hd:(h + 1) * hd] = _rope(
            main[:, nq + h * hd:nq + (h + 1) * hd], ca, sa, hd // 2).astype(BF16)
    v_ref[...] = main[:, nq + nk:nq + 2 * nk].astype(BF16)
    o = nq + 2 * nk
    first = lax.broadcasted_iota(I32, (h_ref.shape[0], LANES), 1) < IDX_DIM
    for s in range(IDX_HEADS * IDX_DIM // LANES):
        pair = _rope(main[:, o + s * LANES:o + (s + 1) * LANES], ci, si, IDX_DIM // 2)
        qi_ref[:, 2 * s * LANES:(2 * s + 1) * LANES] = jnp.where(first, pair, 0.0).astype(BF16)
        qi_ref[:, (2 * s + 1) * LANES:(2 * s + 2) * LANES] = jnp.where(
            first, pltpu.roll(pair, IDX_DIM, axis=1), 0.0).astype(BF16)
    ki_ref[...] = _rope(_dot(n, wk_ref[...]), ci, si, IDX_DIM // 2).astype(BF16)
    wi_ref[...] = _dot(n, ww_ref[...]) * (IDX_HEADS ** -0.5 * IDX_DIM ** -0.5)


def _rope_tables(positions, dim):
    inv = ROPE_THETA ** (-jnp.arange(0, dim, 2, dtype=F32) / dim)
    ang = positions.astype(F32)[..., None] * inv
    cos = jnp.cos(ang)
    sin = jnp.sin(ang)
    reps = LANES // dim
    cos = jnp.tile(jnp.concatenate([cos, cos], -1), (1, 1, reps))
    sin = jnp.tile(jnp.concatenate([-sin, sin], -1), (1, 1, reps))
    return cos.reshape(-1, LANES), sin.reshape(-1, LANES)


def _od_in(h, g, w_in, tables):
    d = h.shape[1]
    nq = ATT_HEADS * ATT_HEAD_DIM
    nk = ATT_KV_HEADS * ATT_HEAD_DIM
    nqi = IDX_HEADS * IDX_DIM
    n_main = nq + 2 * nk + nqi
    w_main = w_in[:, :n_main]
    w_ki = jnp.pad(w_in[:, n_main:n_main + IDX_DIM], ((0, 0), (0, LANES - IDX_DIM)))
    w_wi = jnp.pad(w_in[:, n_main + IDX_DIM:], ((0, 0), (0, LANES - IDX_HEADS)))
    consts = [g.reshape(1, d), w_main.astype(BF16), w_ki.astype(BF16), w_wi.astype(BF16)]
    outs = [(ATT_HEADS, ATT_HEAD_DIM, BF16), (nk, BF16), (nk, BF16), (IDX_HEADS * LANES, BF16),
            (LANES, BF16), (LANES, F32)]
    return _row_tiled(_od_in_body, [h] + list(tables), consts, outs, ROW_TILE, "od_in")


def _sort_key(x):
    bits = lax.bitcast_convert_type(x, I32)
    sign = bits >> 31
    return (bits ^ (sign & jnp.int32(0x7FFFFFFF))) - sign


def _key_value(k):
    sign = k >> 31
    return lax.bitcast_convert_type((k + sign) ^ (sign & jnp.int32(0x7FFFFFFF)), F32)


def _slab_reduce(x, op):
    ways = 4
    parts = [x[r * 8:(r + 1) * 8] for r in range(ways)]
    for r in range(ways, x.shape[0] // 8):
        parts[r % ways] = op(parts[r % ways], x[r * 8:(r + 1) * 8])
    return op(op(parts[0], parts[1]), op(parts[2], parts[3]))


def _sublane_all(x, op):
    for s in (4, 2, 1):
        x = op(x, pltpu.roll(x, s, axis=0))
    return x


def _dsa_body(qi_ref, wi_ref, q_ref, ki_ref, k_ref, vt_ref, o_ref,
              sc_ref, bias_ref, m_ref, acc_ref, *, topk):
    tq = q_ref.shape[1]
    tk = sc_ref.shape[1]
    hd = ATT_HEAD_DIM
    group = ATT_HEADS // ATT_KV_HEADS
    r0 = pl.program_id(1) * tq
    nkb = (r0 + tq + tk - 1) // tk
    qpos = r0 + lax.broadcasted_iota(I32, (1, tq), 1)
    key_ninf = jnp.int32(-2139095040)
    key_lo0 = key_ninf + 1

    def kpos_of(kb):
        return kb * tk + lax.broadcasted_iota(I32, (tk, 1), 0)

    wit = wi_ref[...].T

    def score_block(kb, ext, diagonal):
        kmax, kmin = ext
        c0 = pl.multiple_of(kb * tk, tk)
        kib = ki_ref[pl.ds(c0, tk), :]
        acc = jnp.zeros((tk, tq), F32)
        for h in range(IDX_HEADS):
            lg = lax.dot_general(kib, qi_ref[:, h * LANES:(h + 1) * LANES], _NT,
                                 preferred_element_type=F32)
            acc = acc + jnp.maximum(lg, 0.0) * wit[h:h + 1, :]
        key = _sort_key(acc)
        if diagonal:
            valid = kpos_of(kb) <= qpos
            keys = jnp.where(valid, key, key_ninf)
            lows = jnp.where(valid, key, int_max)
        else:
            keys = lows = key
        sc_ref[kb] = keys
        return (jnp.maximum(kmax, _slab_reduce(keys, jnp.maximum)),
                jnp.minimum(kmin, _slab_reduce(lows, jnp.minimum)))

    int_max = jnp.int32(2147483647)
    n_full = (r0 + 1) // tk
    ext = lax.fori_loop(0, n_full, functools.partial(score_block, diagonal=False),
                        (jnp.full((8, tq), key_ninf, I32), jnp.full((8, tq), int_max, I32)))
    kmax, kmin = lax.fori_loop(n_full, nkb, functools.partial(score_block, diagonal=True), ext)
    kmax = _sublane_all(kmax, jnp.maximum)[0:1]
    kmin = _sublane_all(kmin, jnp.minimum)[0:1]

    def count_ge(cand):
        ways = 4

        def body(kb, parts):
            parts = list(parts)
            for r in range(tk // 8):
                parts[r % ways] = parts[r % ways] + jnp.where(
                    sc_ref[kb, r * 8:(r + 1) * 8, :] >= cand, 1.0, 0.0)
            return tuple(parts)

        parts = lax.fori_loop(0, nkb, body, (jnp.zeros((8, tq), F32),) * ways)
        return _sublane_all((parts[0] + parts[1]) + (parts[2] + parts[3]), jnp.add)[0:1]

    kf = jnp.float32(topk)
    short = (qpos + 1) <= topk

    def search_cond(st):
        return jnp.logical_and(st[0] < 200, st[-1] > 0.0)

    def search_body(st):
        it, lo, hi, c_lo, c_hi, n_acc, side, streak, _ = st
        mixed = jnp.logical_and(lo < 0, hi > 0)
        w = hi - lo
        active = jnp.logical_not(jnp.logical_or(
            short, jnp.logical_or(c_lo == kf, jnp.logical_and(jnp.logical_not(mixed), w <= 1))))
        gal = lax.shift_left(jnp.int32(1 << 22), jnp.clip(it - 1, 0, 8))
        frac = (c_lo - kf - 0.5) / (c_lo - c_hi)
        d_key = (w.astype(F32) * frac).astype(I32)
        v_lo, v_hi = _key_value(lo), _key_value(hi)
        d_val = _sort_key(v_lo + (v_hi - v_lo) * frac) - lo
        neg = hi <= 0
        d = jnp.where(streak >= 2, w >> 1, jnp.where(neg, d_val, d_key))
        d = jnp.where(jnp.logical_and(n_acc == 0, jnp.logical_not(neg)),
                      w - jnp.minimum(gal, w - 1), d)
        cand = jnp.where(mixed, 0, lo + jnp.clip(d, 1, jnp.maximum(w - 1, 1)))
        cnt = count_ge(cand)
        up = jnp.logical_and(active, cnt >= kf)
        dn = jnp.logical_and(active, cnt < kf)
        lo = jnp.where(up, cand, lo)
        c_lo = jnp.where(up, cnt, c_lo)
        hi = jnp.where(dn, cand, hi)
        c_hi = jnp.where(dn, cnt, c_hi)
        n_acc = n_acc + jnp.where(jnp.logical_and(up, jnp.logical_not(mixed)), 1, 0)
        new_side = jnp.where(up, 1, -1)
        streak = jnp.where(new_side == side, streak + 1, 1)
        mixed = jnp.logical_and(lo < 0, hi > 0)
        settled = jnp.logical_or(
            short, jnp.logical_or(c_lo == kf, jnp.logical_and(jnp.logical_not(mixed), hi - lo <= 1)))
        pending = jnp.sum(jnp.where(settled, 0.0, 1.0))
        return it + 1, lo, hi, c_lo, c_hi, n_acc, new_side, streak, pending

    zi = jnp.zeros((1, tq), I32)
    st0 = (jnp.int32(0), kmin, kmax + 1, (qpos + 1).astype(F32), jnp.zeros((1, tq), F32),
           zi, zi, zi, jnp.float32(1.0))
    st = lax.while_loop(search_cond, search_body, st0)
    thr = jnp.where(short, key_lo0, st[1])
    cnt_thr = st[3]

    tied = jnp.logical_and(cnt_thr > kf, jnp.logical_not(short))
    n_tied = jnp.sum(jnp.where(tied, 1.0, 0.0))

    @pl.when(n_tied > 0.0)
    def _():
        need = kf - st[4]
        ltri = jnp.where(lax.broadcasted_iota(I32, (tk, tk), 0) >= lax.broadcasted_iota(I32, (tk, tk), 1),
                         1.0, 0.0).astype(BF16)

        def demote(kb, before):
            keys = sc_ref[kb]
            is_tie = keys == thr
            rank = _dot(ltri, jnp.where(is_tie, 1.0, 0.0).astype(BF16)) + before
            drop = jnp.logical_and(jnp.logical_and(is_tie, rank > need), tied)
            sc_ref[kb] = jnp.where(drop, key_ninf, keys)
            return rank[tk - 1:tk, :]

        lax.fori_loop(0, nkb, demote, jnp.zeros((1, tq), F32))

    def to_bias(kb, carry):
        bias_ref[kb] = jnp.where(sc_ref[kb] >= thr, 0.0, NEG).astype(BF16)
        return carry

    lax.fori_loop(0, nkb, to_bias, 0)

    m_ref[...] = jnp.full_like(m_ref, NEG)
    acc_ref[...] = jnp.zeros_like(acc_ref)
    va = vt_ref.shape[1] // ATT_KV_HEADS

    def attend(kb, carry):
        c0 = pl.multiple_of(kb * tk, tk)
        bias = jnp.concatenate([bias_ref[kb]] * group, axis=1)
        for kv in range(ATT_KV_HEADS):
            kblk = k_ref[pl.ds(c0, tk), kv * hd:(kv + 1) * hd]
            qg = q_ref[kv * group:(kv + 1) * group].reshape(group * tq, hd)
            s = lax.dot_general(kblk, qg, _NT, preferred_element_type=F32).astype(BF16) + bias
            m_old = m_ref[kv:kv + 1, :]
            m_new = jnp.maximum(m_old, jnp.max(s, axis=0, keepdims=True).astype(F32))
            p = jnp.exp2(s - m_new.astype(BF16))
            acc_ref[kv] = jnp.exp2(m_old - m_new) * acc_ref[kv] + _dot(
                vt_ref[kb, kv * va:(kv + 1) * va, :], p)
            m_ref[kv:kv + 1, :] = m_new
        return carry

    lax.fori_loop(0, nkb, attend, 0)
    for h in range(ATT_HEADS):
        kv, cs = h // group, slice((h % group) * tq, (h % group + 1) * tq)
        o_ref[:, h * hd:(h + 1) * hd] = (
            acc_ref[kv, 0:hd, cs] / acc_ref[kv, hd:hd + 1, cs]).T.astype(o_ref.dtype)


def _dsa(q, k, v, qi, ki, wi):
    b, l, nv = v.shape
    nh, _, hd = q.shape
    tq, tk = min(DSA_TQ, l), min(DSA_TK, l)
    topk = min(TOPK_MAX, l // 4)
    group = nh // ATT_KV_HEADS
    vt = jnp.swapaxes(v.reshape(b, l // tk, tk, ATT_KV_HEADS, hd), 2, 4)
    vt = vt.swapaxes(2, 3)
    extra = jnp.zeros((b, l // tk, ATT_KV_HEADS, SUBLANES_BF16, tk), v.dtype).at[:, :, :, 0].set(1.0)
    vt = jnp.concatenate([vt, extra], axis=3)
    va = hd + SUBLANES_BF16
    vt = vt.reshape(b, l // tk, ATT_KV_HEADS * va, tk)

    def qspec(w):
        return pl.BlockSpec((None, tq, w), lambda bi, i: (bi, i, 0))

    def kspec(w):
        return pl.BlockSpec((None, l, w), lambda bi, i: (bi, 0, 0))

    return pl.pallas_call(
        functools.partial(_dsa_body, topk=topk),
        grid=(b, l // tq),
        in_specs=[qspec(qi.shape[2]), qspec(wi.shape[2]),
                  pl.BlockSpec((nh, tq, hd), lambda bi, i: (0, bi * (l // tq) + i, 0)),
                  kspec(ki.shape[2]), kspec(k.shape[2]),
                  pl.BlockSpec((None, l // tk, ATT_KV_HEADS * va, tk), lambda bi, i: (bi, 0, 0, 0))],
        out_specs=qspec(nh * hd),
        out_shape=jax.ShapeDtypeStruct((b, l, nh * hd), BF16),
        scratch_shapes=[pltpu.VMEM((l // tk, tk, tq), I32),
                        pltpu.VMEM((l // tk, tk, tq), BF16),
                        pltpu.VMEM((ATT_KV_HEADS, group * tq), F32),
                        pltpu.VMEM((ATT_KV_HEADS, va, group * tq), F32)],
        compiler_params=_params("parallel", "arbitrary"),
        name="dsa",
    )(qi, wi, q, ki, k, vt)


def _xattn_body(h_ref, g_ref, wq_ref, kv_ref, wo_ref, o_ref):
    x = h_ref[...]
    d = x.shape[1]
    hd = d // XA_HEADS
    n = _rms(x, g_ref[...]).astype(BF16)
    q = (_dot(n, wq_ref[...].astype(BF16)) * (hd ** -0.5)).astype(BF16)
    outs = []
    for h in range(XA_HEADS):
        hs = slice(h * hd, (h + 1) * hd)
        s = lax.dot_general(q[:, hs], kv_ref[:, hs], _NT, preferred_element_type=F32)
        p = jnp.exp(s - jnp.max(s, axis=1, keepdims=True))
        p = p / jnp.sum(p, axis=1, keepdims=True)
        outs.append(_dot(p.astype(BF16), kv_ref[:, d + h * hd:d + (h + 1) * hd]))
    o = jnp.concatenate(outs, axis=1).astype(BF16)
    o_ref[...] = x + _dot(o, wo_ref[...].astype(BF16))


def _xattn(h, g, wq, kv, wo, layer, rows_per_batch):
    m, d = h.shape
    tm = ROW_TILE
    per = rows_per_batch // tm
    return pl.pallas_call(
        _xattn_body,
        grid=(m // tm,),
        in_specs=[pl.BlockSpec((tm, d), lambda i: (i, 0)), _const_spec(g.reshape(1, d)),
                  _const_spec((wq, layer)),
                  pl.BlockSpec((None,) + kv.shape[1:], lambda i: (i // per, 0, 0)),
                  _const_spec((wo, layer))],
        out_specs=pl.BlockSpec((tm, d), lambda i: (i, 0)),
        out_shape=jax.ShapeDtypeStruct((m, d), F32),
        compiler_params=_params("parallel"),
        name="xattn",
    )(h, g.reshape(1, d), wq, kv, wo)


def kernel(x, mem, positions, ffn1_norm, ffn1_w_gate, ffn1_w_up, ffn1_w_down, mix_norm, ev_w_in, ev_w_out, gla_w_alpha, gla_b_alpha, gla_norm, s5_lambda_re, s5_lambda_im, s5_log_dt, s5_B_re, s5_B_im, s5_C_re, s5_C_im, s5_D, s5_w_glu, s5_b_glu, od_w_in, od_w_out, xa_norm, mem_norm, xa_wq, xa_wk, xa_wv, xa_wo, ffn2_norm, ffn2_w_gate, ffn2_w_up, ffn2_w_down, final_norm):
    b, l, d = x.shape
    n_mem = mem.shape[1]
    depth = ffn1_norm.shape[0]
    tables = _rope_tables(positions, ATT_HEAD_DIM) + _rope_tables(positions, IDX_DIM)
    mem2 = mem.reshape(b * n_mem, d)
    h = x.reshape(b * l, d)

    def seq(a):
        return a.reshape(b, l, a.shape[-1])

    def flat(a):
        return a.reshape(b * l, a.shape[-1])

    for layer in range(depth):
        h = _ffn(h, ffn1_norm[layer], ffn1_w_gate, ffn1_w_up, ffn1_w_down, layer)
        if layer % 2 == 0:
            e = layer // 2
            qk, v, r, la, u = _ev_in(h, mix_norm[layer], ev_w_in[e], gla_w_alpha[e], gla_b_alpha[e])
            o_gla = _gla(seq(qk), seq(v), seq(r), seq(la), gla_norm[e])
            o_s5 = _s5(seq(u), s5_lambda_re[e], s5_lambda_im[e], s5_log_dt[e], s5_B_re[e],
                       s5_B_im[e], s5_C_re[e], s5_C_im[e], s5_D[e], s5_w_glu[e], s5_b_glu[e])
            h = _proj_res(h, [flat(o_gla), flat(o_s5)], ev_w_out, e)
        else:
            o = layer // 2
            q, k, v, qi, ki, wi = _od_in(h, mix_norm[layer], od_w_in[o], tables)
            att = _dsa(q, seq(k), seq(v), seq(qi), seq(ki), seq(wi))
            h = _proj_res(h, [flat(att)], od_w_out, o)
        kv = _norm_proj(mem2, mem_norm, jnp.concatenate([xa_wk[layer], xa_wv[layer]], axis=1),
                        n_mem, BF16)
        h = _xattn(h, xa_norm[layer], xa_wq, kv.reshape(b, n_mem, 2 * d), xa_wo, layer, l)
        last = layer == depth - 1
        h = _ffn(h, ffn2_norm[layer], ffn2_w_gate, ffn2_w_up, ffn2_w_down, layer,
                 final_g=final_norm if last else None)
    return h.reshape(b, l, d)
```

```python
import functools
import math

import jax
import jax.numpy as jnp
from jax import lax
from jax.experimental import pallas as pl
from jax.experimental.pallas import tpu as pltpu

F32 = jnp.float32
BF16 = jnp.bfloat16
I32 = jnp.int32

EPS = 1e-6
ROPE_THETA = 10000.0
GLA_HEADS, GLA_DK, GLA_DV, GLA_RANK, GLA_TAU, GLA_CHUNK = 4, 64, 128, 16, 16.0, 64
S5_GROUP, S5_STATE = 16, 64
ATT_HEADS, ATT_KV_HEADS, ATT_HEAD_DIM = 8, 2, 128
IDX_HEADS, IDX_DIM = 8, 64
TOPK_MAX = 256
XA_HEADS = 4

LANES = 128
VMEM_LIMIT = 56 * 1024 * 1024
ROW_TILE = 512
FFN_CHUNK = 256
GLA_TILE = 256
S5_TILE = 256
S5_BLOCK = 8
S5_SPLIT = 4
DSA_TQ = 256
DSA_TK = 512
SEARCH_MIN_PASSES = 12
NEG = -(2.0 ** 100)
SUBLANES_BF16 = 16

_NT = (((1,), (1,)), ((), ()))
_TN = (((0,), (0,)), ((), ()))


def _params(*sem):
    return pltpu.CompilerParams(dimension_semantics=sem, vmem_limit_bytes=VMEM_LIMIT)


def _const_spec(a):
    if isinstance(a, tuple):
        a, layer = a
        nd = a.ndim - 1
        return pl.BlockSpec((None,) + a.shape[1:], lambda *_: (layer,) + (0,) * nd,
                            pipeline_mode=pl.Buffered(1))
    nd = a.ndim
    return pl.BlockSpec(a.shape, lambda *_: (0,) * nd, pipeline_mode=pl.Buffered(1))


def _rms(x, g):
    return x * lax.rsqrt(jnp.mean(x * x, axis=-1, keepdims=True) + EPS) * g


def _dot(a, b):
    return jnp.dot(a, b, preferred_element_type=F32)


def _row_tiled(body, rows, consts, outs, tm, name):
    m = rows[0].shape[0]
    assert m % tm == 0

    def out_desc(o):
        if len(o) == 2:
            return pl.BlockSpec((tm, o[0]), lambda i: (i, 0)), jax.ShapeDtypeStruct((m, o[0]), o[1])
        return (pl.BlockSpec((o[0], tm, o[1]), lambda i: (0, i, 0)),
                jax.ShapeDtypeStruct((o[0], m, o[1]), o[2]))

    descs = [out_desc(o) for o in outs]
    return pl.pallas_call(
        body,
        grid=(m // tm,),
        in_specs=[pl.BlockSpec((tm, r.shape[1]), lambda i: (i, 0)) for r in rows]
        + [_const_spec(c) for c in consts],
        out_specs=[d[0] for d in descs],
        out_shape=[d[1] for d in descs],
        compiler_params=_params("parallel"),
        name=name,
    )(*rows, *[c[0] if isinstance(c, tuple) else c for c in consts])


def _ffn_body(h_ref, g_ref, wg_ref, wu_ref, wd_ref, *rest, final):
    o_ref = rest[-1]
    x = h_ref[...]
    n = _rms(x, g_ref[...]).astype(BF16)
    acc = jnp.zeros_like(x)
    for c in range(wg_ref.shape[1] // FFN_CHUNK):
        sl = slice(c * FFN_CHUNK, (c + 1) * FFN_CHUNK)
        g = _dot(n, wg_ref[:, sl].astype(BF16))
        u = _dot(n, wu_ref[:, sl].astype(BF16))
        a = (g * jax.nn.sigmoid(g) * u).astype(BF16)
        acc = acc + _dot(a, wd_ref[sl, :].astype(BF16))
    y = x + 0.5 * acc
    if final:
        y = _rms(y, rest[0][...])
    o_ref[...] = y


def _ffn(h, g, wg, wu, wd, layer, final_g=None):
    d = h.shape[1]
    consts = [g.reshape(1, d), (wg, layer), (wu, layer), (wd, layer)]
    if final_g is not None:
        consts.append(final_g.reshape(1, d))
    body = functools.partial(_ffn_body, final=final_g is not None)
    return _row_tiled(body, [h], consts, [(d, F32)], ROW_TILE, "ffn")[0]


def _proj_res_body(*refs):
    h_ref, w_ref, o_ref = refs[0], refs[-2], refs[-1]
    acc = h_ref[...]
    r0 = 0
    for x_ref in refs[1:-2]:
        k = x_ref.shape[1]
        acc = acc + _dot(x_ref[...].astype(BF16), w_ref[r0:r0 + k, :].astype(BF16))
        r0 += k
    o_ref[...] = acc


def _proj_res(h, xs, w_stack, layer):
    return _row_tiled(_proj_res_body, [h] + list(xs), [(w_stack, layer)],
                      [(h.shape[1], F32)], ROW_TILE, "proj_res")[0]


def _norm_proj_body(x_ref, g_ref, w_ref, o_ref):
    n = _rms(x_ref[...], g_ref[...]).astype(BF16)
    o_ref[...] = _dot(n, w_ref[...]).astype(o_ref.dtype)


def _norm_proj(x, g, w, tm, out_dtype):
    return _row_tiled(_norm_proj_body, [x], [g.reshape(1, -1), w.astype(BF16)],
                      [(w.shape[1], out_dtype)], tm, "norm_proj")[0]


def _log_sigmoid(x):
    return jnp.minimum(x, 0.0) - jnp.log(1.0 + jnp.exp(-jnp.abs(x)))


def _ev_in_body(h_ref, g_ref, wm_ref, wa_ref, wa2_ref, ba_ref, wu_ref,
                qk_ref, v_ref, r_ref, la_ref, u_ref):
    n = _rms(h_ref[...], g_ref[...]).astype(BF16)
    nqk = qk_ref.shape[1]
    nv = v_ref.shape[1]
    main = _dot(n, wm_ref[...])
    qk_ref[...] = main[:, :nqk]
    v_ref[...] = main[:, nqk:nqk + nv]
    r_ref[...] = main[:, nqk + nv:]
    a_low = _dot(n, wa_ref[...]).astype(BF16)
    alpha = _dot(a_low, wa2_ref[...]) + ba_ref[...]
    la_ref[...] = _log_sigmoid(alpha) / GLA_TAU
    u_ref[...] = _dot(n, wu_ref[...])


def _ev_in(h, g, w_in, w_a2, b_a):
    d = h.shape[1]
    nqk = 2 * GLA_HEADS * GLA_DK
    nv = GLA_HEADS * GLA_DV
    n_main = nqk + 2 * nv
    w_main = w_in[:, :n_main]
    w_a = jnp.pad(w_in[:, n_main:n_main + GLA_RANK], ((0, 0), (0, LANES - GLA_RANK)))
    w_u = w_in[:, n_main + GLA_RANK:]
    w_a2p = jnp.pad(w_a2, ((0, LANES - GLA_RANK), (0, 0)))
    consts = [g.reshape(1, d), w_main.astype(BF16), w_a.astype(BF16), w_a2p.astype(BF16),
              b_a.reshape(1, -1), w_u.astype(BF16)]
    outs = [(nqk, F32), (nv, F32), (nv, F32), (GLA_HEADS * GLA_DK, F32), (w_u.shape[1], F32)]
    return _row_tiled(_ev_in_body, [h], consts, outs, ROW_TILE, "ev_in")


def _gla_body(qk_ref, v_ref, r_ref, la_ref, gn_ref, o_ref, st_ref):
    tl = qk_ref.shape[0]
    hdk = GLA_HEADS * GLA_DK
    c = GLA_CHUNK

    @pl.when(pl.program_id(1) == 0)
    def _():
        st_ref[...] = jnp.zeros_like(st_ref)

    row = lax.broadcasted_iota(I32, (tl, hdk), 0) % c
    bc = la_ref[...]
    d = 1
    while d < c:
        bc = bc + jnp.where(row >= d, pltpu.roll(bc, d, axis=0), 0.0)
        d *= 2
    q_dec = qk_ref[:, :hdk] * (GLA_DK ** -0.5) * jnp.exp(bc)
    k = qk_ref[:, hdk:]
    k_inv = k * jnp.exp(-bc)
    causal = lax.broadcasted_iota(I32, (c, c), 0) >= lax.broadcasted_iota(I32, (c, c), 1)
    gn = gn_ref[...]
    for j in range(tl // c):
        rs = slice(j * c, (j + 1) * c)
        b_last = bc[j * c + c - 1:j * c + c, :]
        k_end = k[rs] * jnp.exp(b_last - bc[rs])
        dec = jnp.exp(b_last)
        for h in range(GLA_HEADS):
            ks = slice(h * GLA_DK, (h + 1) * GLA_DK)
            vs = slice(h * GLA_DV, (h + 1) * GLA_DV)
            qd = q_dec[rs, ks].astype(BF16)
            vh = v_ref[rs, vs].astype(BF16)
            s = lax.dot_general(qd, k_inv[rs, ks].astype(BF16), _NT, preferred_element_type=F32)
            s = jnp.where(causal, s, 0.0).astype(BF16)
            st = st_ref[h]
            o = _dot(s, vh) + lax.dot_general(qd, st.astype(BF16), _NT, preferred_element_type=F32)
            st_ref[h] = st * dec[:, ks] + lax.dot_general(
                vh, k_end[:, ks].astype(BF16), _TN, preferred_element_type=F32)
            o = o * lax.rsqrt(jnp.mean(o * o, axis=-1, keepdims=True) + EPS) * gn
            r = r_ref[rs, vs]
            o_ref[rs, vs] = o * (r * jax.nn.sigmoid(r))


def _gla(qk, v, r, la, gn):
    b, l, _ = qk.shape
    tl = GLA_TILE

    def spec(w):
        return pl.BlockSpec((None, tl, w), lambda bi, t: (bi, t, 0))

    return pl.pallas_call(
        _gla_body,
        grid=(b, l // tl),
        in_specs=[spec(qk.shape[2]), spec(v.shape[2]), spec(r.shape[2]), spec(la.shape[2]),
                  pl.BlockSpec((1, GLA_DV), lambda bi, t: (0, 0))],
        out_specs=spec(v.shape[2]),
        out_shape=jax.ShapeDtypeStruct(v.shape, F32),
        scratch_shapes=[pltpu.VMEM((GLA_HEADS, GLA_DV, GLA_DK), F32)],
        compiler_params=_params("parallel", "arbitrary"),
        name="gla",
    )(qk, v, r, la, gn.reshape(1, GLA_DV))


def _cmul(ar, ai, br, bi):
    return ar * br - ai * bi, ar * bi + ai * br


def _s5_prep_body(lr_ref, li_ref, ldt_ref, br_ref, bi_ref, bbr_ref, bbi_ref, pr_ref, pi_ref):
    lr = jnp.minimum(lr_ref[...], -1e-4)
    li = li_ref[...]
    dt = jnp.exp(ldt_ref[...])
    mag = jnp.exp(lr * dt)
    ar = mag * jnp.cos(li * dt)
    ai = mag * jnp.sin(li * dt)
    den = lr * lr + li * li
    nr = ar - 1.0
    fr = (nr * lr + ai * li) / den
    fi = (ai * lr - nr * li) / den
    w = br_ref.shape[2]
    for j in range(br_ref.shape[0]):
        sl = slice(j * w, (j + 1) * w)
        b_re, b_im = br_ref[j], bi_ref[j]
        bbr_ref[j] = (fr[:, sl] * b_re - fi[:, sl] * b_im).astype(BF16)
        bbi_ref[j] = (fr[:, sl] * b_im + fi[:, sl] * b_re).astype(BF16)
    pr_ref[0:1, :] = ar
    pi_ref[0:1, :] = ai
    n = 1
    while n < pr_ref.shape[0]:
        sr, si = pr_ref[n - 1:n, :], pi_ref[n - 1:n, :]
        nr_, ni_ = _cmul(pr_ref[0:n, :], pi_ref[0:n, :], sr, si)
        pr_ref[n:2 * n, :] = nr_
        pi_ref[n:2 * n, :] = ni_
        n *= 2


def _s5_body(u_ref, bbr_ref, bbi_ref, pr_ref, pi_ref, cr_ref, ci_ref, d_ref, wg_ref, bg_ref,
             o_ref, sr_ref, si_ref):
    t = u_ref.shape[0]
    w = bbr_ref.shape[2]
    cw = bbr_ref.shape[1]

    @pl.when(pl.program_id(1) == 0)
    def _():
        sr_ref[...] = jnp.zeros_like(sr_ref)
        si_ref[...] = jnp.zeros_like(si_ref)

    u = u_ref[...]
    ub = u.astype(BF16)
    nb = pr_ref.shape[0]
    row = lax.broadcasted_iota(I32, (1, nb, w), 1)
    ys = []
    for j in range(bbr_ref.shape[0]):
        ls = slice(j * w, (j + 1) * w)
        uj = ub[:, j * cw:(j + 1) * cw]
        xr = _dot(uj, bbr_ref[j]).reshape(t // nb, nb, w)
        xi = _dot(uj, bbi_ref[j]).reshape(t // nb, nb, w)
        d = 1
        while d < nb:
            ar = jnp.where(row >= d, pr_ref[d - 1:d, ls], 0.0)
            ai = jnp.where(row >= d, pi_ref[d - 1:d, ls], 0.0)
            dr, di = _cmul(ar, ai, pltpu.roll(xr, d, axis=1), pltpu.roll(xi, d, axis=1))
            xr, xi = xr + dr, xi + di
            d *= 2
        xr, xi = xr.reshape(t, w), xi.reshape(t, w)
        p_r, p_i = pr_ref[:, ls], pi_ref[:, ls]
        c_r, c_i = sr_ref[:, ls], si_ref[:, ls]
        blocks_r, blocks_i = [], []
        for blk in range(t // nb):
            dr, di = _cmul(p_r, p_i, c_r, c_i)
            b_r = xr[blk * nb:(blk + 1) * nb] + dr
            b_i = xi[blk * nb:(blk + 1) * nb] + di
            c_r, c_i = b_r[nb - 1:nb], b_i[nb - 1:nb]
            blocks_r.append(b_r)
            blocks_i.append(b_i)
        sr_ref[:, ls] = c_r
        si_ref[:, ls] = c_i
        xr = jnp.concatenate(blocks_r, axis=0)
        xi = jnp.concatenate(blocks_i, axis=0)
        ys.append(_dot(xr.astype(BF16), cr_ref[j]) - _dot(xi.astype(BF16), ci_ref[j]))
    y = jnp.concatenate(ys, axis=1) + d_ref[...] * u
    y = jax.nn.gelu(y)
    gate = jax.nn.sigmoid(_dot(y.astype(BF16), wg_ref[...]) + bg_ref[...])
    o_ref[...] = y * gate


def _block_diag(m, split):
    g, r, c = m.shape
    gs = g // split
    eye = jnp.eye(gs, dtype=m.dtype)
    m = m.reshape(split, gs, r, c)
    out = m[:, :, :, None, :] * eye[None, :, None, :, None]
    return out.reshape(split, gs * r, gs * c)


def _s5(u, lam_re, lam_im, log_dt, b_re, b_im, c_re, c_im, dd, w_glu, b_glu):
    b, l, width = u.shape
    g, p = lam_re.shape
    n_state = g * p
    t = S5_TILE
    br_bd = _block_diag(jnp.swapaxes(b_re, 1, 2), S5_SPLIT)
    bi_bd = _block_diag(jnp.swapaxes(b_im, 1, 2), S5_SPLIT)
    cr_bd = _block_diag(jnp.swapaxes(c_re, 1, 2), S5_SPLIT).astype(BF16)
    ci_bd = _block_diag(jnp.swapaxes(c_im, 1, 2), S5_SPLIT).astype(BF16)
    ldt = jnp.broadcast_to(log_dt[:, None], (g, p)).reshape(1, n_state)
    bbr, bbi, pw_r, pw_i = pl.pallas_call(
        _s5_prep_body,
        out_shape=[jax.ShapeDtypeStruct(br_bd.shape, BF16), jax.ShapeDtypeStruct(br_bd.shape, BF16),
                   jax.ShapeDtypeStruct((S5_BLOCK, n_state), F32),
                   jax.ShapeDtypeStruct((S5_BLOCK, n_state), F32)],
        compiler_params=pltpu.CompilerParams(vmem_limit_bytes=VMEM_LIMIT),
        name="s5_prep",
    )(lam_re.reshape(1, n_state), lam_im.reshape(1, n_state), ldt, br_bd, bi_bd)

    consts = [bbr, bbi, pw_r, pw_i, cr_bd, ci_bd, dd.reshape(1, width), w_glu.astype(BF16),
              b_glu.reshape(1, width)]
    return pl.pallas_call(
        _s5_body,
        grid=(b, l // t),
        in_specs=[pl.BlockSpec((None, t, width), lambda bi_, ti: (bi_, ti, 0))]
        + [_const_spec(c) for c in consts],
        out_specs=pl.BlockSpec((None, t, width), lambda bi_, ti: (bi_, ti, 0)),
        out_shape=jax.ShapeDtypeStruct(u.shape, F32),
        scratch_shapes=[pltpu.VMEM((1, n_state), F32), pltpu.VMEM((1, n_state), F32)],
        compiler_params=_params("parallel", "arbitrary"),
        name="s5",
    )(u, *consts)


def _rope(x, cos, sin_signed, half):
    lane = lax.broadcasted_iota(I32, x.shape, 1) % (2 * half)
    rot = jnp.where(lane < half, pltpu.roll(x, LANES - half, axis=1), pltpu.roll(x, half, axis=1))
    return x * cos + rot * sin_signed


def _od_in_body(h_ref, ca_ref, sa_ref, ci_ref, si_ref, g_ref, wm_ref, wk_ref, ww_ref,
                q_ref, k_ref, v_ref, qi_ref, ki_ref, wi_ref):
    n = _rms(h_ref[...], g_ref[...]).astype(BF16)
    main = _dot(n, wm_ref[...])
    ca, sa, ci, si = ca_ref[...], sa_ref[...], ci_ref[...], si_ref[...]
    hd = ATT_HEAD_DIM
    nq, nk = q_ref.shape[0] * hd, k_ref.shape[1]
    for h in range(nq // hd):
        sl = slice(h * hd, (h + 1) * hd)
        q_ref[h] = (_rope(main[:, sl], ca, sa, hd // 2) * (hd ** -0.5 * math.log2(math.e))).astype(BF16)
    for h in range(nk // hd):
        k_ref[:, h * hd:(h + 1) * hd] = _rope(
            main[:, nq + h * hd:nq + (h + 1) * hd], ca, sa, hd // 2).astype(BF16)
    v_ref[...] = main[:, nq + nk:nq + 2 * nk].astype(BF16)
    o = nq + 2 * nk
    first = lax.broadcasted_iota(I32, (h_ref.shape[0], LANES), 1) < IDX_DIM
    for s in range(IDX_HEADS * IDX_DIM // LANES):
        pair = _rope(main[:, o + s * LANES:o + (s + 1) * LANES], ci, si, IDX_DIM // 2)
        qi_ref[:, 2 * s * LANES:(2 * s + 1) * LANES] = jnp.where(first, pair, 0.0).astype(BF16)
        qi_ref[:, (2 * s + 1) * LANES:(2 * s + 2) * LANES] = jnp.where(
            first, pltpu.roll(pair, IDX_DIM, axis=1), 0.0).astype(BF16)
    ki_ref[...] = _rope(_dot(n, wk_ref[...]), ci, si, IDX_DIM // 2).astype(BF16)
    wi_ref[...] = _dot(n, ww_ref[...]) * (IDX_HEADS ** -0.5 * IDX_DIM ** -0.5)


def _rope_tables(positions, dim):
    inv = ROPE_THETA ** (-jnp.arange(0, dim, 2, dtype=F32) / dim)
    ang = positions.astype(F32)[..., None] * inv
    cos = jnp.cos(ang)
    sin = jnp.sin(ang)
    reps = LANES // dim
    cos = jnp.tile(jnp.concatenate([cos, cos], -1), (1, 1, reps))
    sin = jnp.tile(jnp.concatenate([-sin, sin], -1), (1, 1, reps))
    return cos.reshape(-1, LANES), sin.reshape(-1, LANES)


def _od_in(h, g, w_in, tables):
    d = h.shape[1]
    nq = ATT_HEADS * ATT_HEAD_DIM
    nk = ATT_KV_HEADS * ATT_HEAD_DIM
    nqi = IDX_HEADS * IDX_DIM
    n_main = nq + 2 * nk + nqi
    w_main = w_in[:, :n_main]
    w_ki = jnp.pad(w_in[:, n_main:n_main + IDX_DIM], ((0, 0), (0, LANES - IDX_DIM)))
    w_wi = jnp.pad(w_in[:, n_main + IDX_DIM:], ((0, 0), (0, LANES - IDX_HEADS)))
    consts = [g.reshape(1, d), w_main.astype(BF16), w_ki.astype(BF16), w_wi.astype(BF16)]
    outs = [(ATT_HEADS, ATT_HEAD_DIM, BF16), (nk, BF16), (nk, BF16), (IDX_HEADS * LANES, BF16),
            (LANES, BF16), (LANES, F32)]
    return _row_tiled(_od_in_body, [h] + list(tables), consts, outs, ROW_TILE, "od_in")


def _sort_key(x):
    bits = lax.bitcast_convert_type(x, I32)
    sign = bits >> 31
    return (bits ^ (sign & jnp.int32(0x7FFFFFFF))) - sign


def _key_value(k):
    sign = k >> 31
    return lax.bitcast_convert_type((k + sign) ^ (sign & jnp.int32(0x7FFFFFFF)), F32)


def _slab_reduce(x, op):
    ways = 4
    parts = [x[r * 8:(r + 1) * 8] for r in range(ways)]
    for r in range(ways, x.shape[0] // 8):
        parts[r % ways] = op(parts[r % ways], x[r * 8:(r + 1) * 8])
    return op(op(parts[0], parts[1]), op(parts[2], parts[3]))


def _sublane_all(x, op):
    for s in (4, 2, 1):
        x = op(x, pltpu.roll(x, s, axis=0))
    return x


def _dsa_body(qi_ref, wi_ref, q_ref, ki_ref, k_ref, vt_ref, o_ref,
              sc_ref, bias_ref, m_ref, acc_ref, *, topk):
    tq = q_ref.shape[1]
    tk = sc_ref.shape[1]
    hd = ATT_HEAD_DIM
    group = ATT_HEADS // ATT_KV_HEADS
    r0 = pl.program_id(1) * tq
    nkb = (r0 + tq + tk - 1) // tk
    qpos = r0 + lax.broadcasted_iota(I32, (1, tq), 1)
    key_ninf = jnp.int32(-2139095040)
    key_lo0 = key_ninf + 1

    def kpos_of(kb):
        return kb * tk + lax.broadcasted_iota(I32, (tk, 1), 0)

    wit = wi_ref[...].T

    def score_block(kb, ext, diagonal):
        kmax, kmin = ext
        c0 = pl.multiple_of(kb * tk, tk)
        kib = ki_ref[pl.ds(c0, tk), :]
        acc = jnp.zeros((tk, tq), F32)
        for h in range(IDX_HEADS):
            lg = lax.dot_general(kib, qi_ref[:, h * LANES:(h + 1) * LANES], _NT,
                                 preferred_element_type=F32)
            acc = acc + jnp.maximum(lg, 0.0) * wit[h:h + 1, :]
        key = _sort_key(acc)
        if diagonal:
            valid = kpos_of(kb) <= qpos
            keys = jnp.where(valid, key, key_ninf)
            lows = jnp.where(valid, key, int_max)
        else:
            keys = lows = key
        sc_ref[kb] = keys
        return (jnp.maximum(kmax, _slab_reduce(keys, jnp.maximum)),
                jnp.minimum(kmin, _slab_reduce(lows, jnp.minimum)))

    int_max = jnp.int32(2147483647)
    n_full = (r0 + 1) // tk
    ext = lax.fori_loop(0, n_full, functools.partial(score_block, diagonal=False),
                        (jnp.full((8, tq), key_ninf, I32), jnp.full((8, tq), int_max, I32)))
    kmax, kmin = lax.fori_loop(n_full, nkb, functools.partial(score_block, diagonal=True), ext)
    kmax = _sublane_all(kmax, jnp.maximum)[0:1]
    kmin = _sublane_all(kmin, jnp.minimum)[0:1]

    def count_ge(cand):
        ways = 4

        def body(kb, parts):
            parts = list(parts)
            for r in range(tk // 8):
                parts[r % ways] = parts[r % ways] + jnp.where(
                    sc_ref[kb, r * 8:(r + 1) * 8, :] >= cand, 1.0, 0.0)
            return tuple(parts)

        parts = lax.fori_loop(0, nkb, body, (jnp.zeros((8, tq), F32),) * ways)
        return _sublane_all((parts[0] + parts[1]) + (parts[2] + parts[3]), jnp.add)[0:1]

    kf = jnp.float32(topk)
    short = (qpos + 1) <= topk

    def search_body(st):
        it, lo, hi, c_lo, c_hi, n_acc, side, streak = st
        mixed = jnp.logical_and(lo < 0, hi > 0)
        w = hi - lo
        active = jnp.logical_not(jnp.logical_or(
            short, jnp.logical_or(c_lo == kf, jnp.logical_and(jnp.logical_not(mixed), w <= 1))))
        gal = lax.shift_left(jnp.int32(1 << 22), jnp.clip(it - 1, 0, 8))
        frac = (c_lo - kf - 0.5) / (c_lo - c_hi)
        d_key = (w.astype(F32) * frac).astype(I32)
        v_lo, v_hi = _key_value(lo), _key_value(hi)
        d_val = _sort_key(v_lo + (v_hi - v_lo) * frac) - lo
        neg = hi <= 0
        d = jnp.where(streak >= 2, w >> 1, jnp.where(neg, d_val, d_key))
        d = jnp.where(jnp.logical_and(n_acc == 0, jnp.logical_not(neg)),
                      w - jnp.minimum(gal, w - 1), d)
        cand = jnp.where(mixed, 0, lo + jnp.clip(d, 1, jnp.maximum(w - 1, 1)))
        cnt = count_ge(cand)
        up = jnp.logical_and(active, cnt >= kf)
        dn = jnp.logical_and(active, cnt < kf)
        lo = jnp.where(up, cand, lo)
        c_lo = jnp.where(up, cnt, c_lo)
        hi = jnp.where(dn, cand, hi)
        c_hi = jnp.where(dn, cnt, c_hi)
        n_acc = n_acc + jnp.where(jnp.logical_and(up, jnp.logical_not(mixed)), 1, 0)
        new_side = jnp.where(up, 1, -1)
        streak = jnp.where(new_side == side, streak + 1, 1)
        return it + 1, lo, hi, c_lo, c_hi, n_acc, new_side, streak

    def pending(st):
        _, lo, hi, c_lo = st[:4]
        mixed = jnp.logical_and(lo < 0, hi > 0)
        settled = jnp.logical_or(
            short, jnp.logical_or(c_lo == kf, jnp.logical_and(jnp.logical_not(mixed), hi - lo <= 1)))
        return jnp.sum(jnp.where(settled, 0.0, 1.0))

    zi = jnp.zeros((1, tq), I32)
    st = (jnp.int32(0), kmin, kmax + 1, (qpos + 1).astype(F32), jnp.zeros((1, tq), F32), zi, zi, zi)
    st = lax.fori_loop(0, SEARCH_MIN_PASSES, lambda _, s: search_body(s), st)
    st, _ = lax.while_loop(
        lambda c: jnp.logical_and(c[0][0] < 200, c[1] > 0.0),
        lambda c: (lambda s: (s, pending(s)))(search_body(search_body(c[0]))),
        (st, pending(st)))
    thr = jnp.where(short, key_lo0, st[1])
    cnt_thr = st[3]

    tied = jnp.logical_and(cnt_thr > kf, jnp.logical_not(short))
    n_tied = jnp.sum(jnp.where(tied, 1.0, 0.0))

    @pl.when(n_tied > 0.0)
    def _():
        need = kf - st[4]
        ltri = jnp.where(lax.broadcasted_iota(I32, (tk, tk), 0) >= lax.broadcasted_iota(I32, (tk, tk), 1),
                         1.0, 0.0).astype(BF16)

        def demote(kb, before):
            keys = sc_ref[kb]
            is_tie = keys == thr
            rank = _dot(ltri, jnp.where(is_tie, 1.0, 0.0).astype(BF16)) + before
            drop = jnp.logical_and(jnp.logical_and(is_tie, rank > need), tied)
            sc_ref[kb] = jnp.where(drop, key_ninf, keys)
            return rank[tk - 1:tk, :]

        lax.fori_loop(0, nkb, demote, jnp.zeros((1, tq), F32))

    def to_bias(kb, carry):
        bias_ref[kb] = jnp.where(sc_ref[kb] >= thr, 0.0, NEG).astype(BF16)
        return carry

    lax.fori_loop(0, nkb, to_bias, 0)

    m_ref[...] = jnp.full_like(m_ref, NEG)
    acc_ref[...] = jnp.zeros_like(acc_ref)
    va = vt_ref.shape[1] // ATT_KV_HEADS

    def attend(kb, carry):
        c0 = pl.multiple_of(kb * tk, tk)
        bias = jnp.concatenate([bias_ref[kb]] * group, axis=1)
        for kv in range(ATT_KV_HEADS):
            kblk = k_ref[pl.ds(c0, tk), kv * hd:(kv + 1) * hd]
            qg = q_ref[kv * group:(kv + 1) * group].reshape(group * tq, hd)
            s = lax.dot_general(kblk, qg, _NT, preferred_element_type=F32).astype(BF16) + bias
            m_old = m_ref[kv:kv + 1, :]
            m_new = jnp.maximum(m_old, jnp.max(s, axis=0, keepdims=True).astype(F32))
            p = jnp.exp2(s - m_new.astype(BF16))
            acc_ref[kv] = jnp.exp2(m_old - m_new) * acc_ref[kv] + _dot(
                vt_ref[kb, kv * va:(kv + 1) * va, :], p)
            m_ref[kv:kv + 1, :] = m_new
        return carry

    lax.fori_loop(0, nkb, attend, 0)
    for h in range(ATT_HEADS):
        kv, cs = h // group, slice((h % group) * tq, (h % group + 1) * tq)
        o_ref[:, h * hd:(h + 1) * hd] = (
            acc_ref[kv, 0:hd, cs] / acc_ref[kv, hd:hd + 1, cs]).T.astype(o_ref.dtype)


def _dsa(q, k, v, qi, ki, wi):
    b, l, nv = v.shape
    nh, _, hd = q.shape
    tq, tk = min(DSA_TQ, l), min(DSA_TK, l)
    topk = min(TOPK_MAX, l // 4)
    group = nh // ATT_KV_HEADS
    vt = jnp.swapaxes(v.reshape(b, l // tk, tk, ATT_KV_HEADS, hd), 2, 4)
    vt = vt.swapaxes(2, 3)
    extra = jnp.zeros((b, l // tk, ATT_KV_HEADS, SUBLANES_BF16, tk), v.dtype).at[:, :, :, 0].set(1.0)
    vt = jnp.concatenate([vt, extra], axis=3)
    va = hd + SUBLANES_BF16
    vt = vt.reshape(b, l // tk, ATT_KV_HEADS * va, tk)

    def qspec(w):
        return pl.BlockSpec((None, tq, w), lambda bi, i: (bi, i, 0))

    def kspec(w):
        return pl.BlockSpec((None, l, w), lambda bi, i: (bi, 0, 0))

    return pl.pallas_call(
        functools.partial(_dsa_body, topk=topk),
        grid=(b, l // tq),
        in_specs=[qspec(qi.shape[2]), qspec(wi.shape[2]),
                  pl.BlockSpec((nh, tq, hd), lambda bi, i: (0, bi * (l // tq) + i, 0)),
                  kspec(ki.shape[2]), kspec(k.shape[2]),
                  pl.BlockSpec((None, l // tk, ATT_KV_HEADS * va, tk), lambda bi, i: (bi, 0, 0, 0))],
        out_specs=qspec(nh * hd),
        out_shape=jax.ShapeDtypeStruct((b, l, nh * hd), BF16),
        scratch_shapes=[pltpu.VMEM((l // tk, tk, tq), I32),
                        pltpu.VMEM((l // tk, tk, tq), BF16),
                        pltpu.VMEM((ATT_KV_HEADS, group * tq), F32),
                        pltpu.VMEM((ATT_KV_HEADS, va, group * tq), F32)],
        compiler_params=_params("parallel", "arbitrary"),
        name="dsa",
    )(qi, wi, q, ki, k, vt)


def _xattn_body(h_ref, g_ref, wq_ref, kv_ref, wo_ref, o_ref):
    x = h_ref[...]
    d = x.shape[1]
    hd = d // XA_HEADS
    n = _rms(x, g_ref[...]).astype(BF16)
    q = (_dot(n, wq_ref[...].astype(BF16)) * (hd ** -0.5)).astype(BF16)
    outs = []
    for h in range(XA_HEADS):
        hs = slice(h * hd, (h + 1) * hd)
        s = lax.dot_general(q[:, hs], kv_ref[:, hs], _NT, preferred_element_type=F32)
        p = jnp.exp(s - jnp.max(s, axis=1, keepdims=True))
        p = p / jnp.sum(p, axis=1, keepdims=True)
        outs.append(_dot(p.astype(BF16), kv_ref[:, d + h * hd:d + (h + 1) * hd]))
    o = jnp.concatenate(outs, axis=1).astype(BF16)
    o_ref[...] = x + _dot(o, wo_ref[...].astype(BF16))


def _xattn(h, g, wq, kv, wo, layer, rows_per_batch):
    m, d = h.shape
    tm = ROW_TILE
    per = rows_per_batch // tm
    return pl.pallas_call(
        _xattn_body,
        grid=(m // tm,),
        in_specs=[pl.BlockSpec((tm, d), lambda i: (i, 0)), _const_spec(g.reshape(1, d)),
                  _const_spec((wq, layer)),
                  pl.BlockSpec((None,) + kv.shape[1:], lambda i: (i // per, 0, 0)),
                  _const_spec((wo, layer))],
        out_specs=pl.BlockSpec((tm, d), lambda i: (i, 0)),
        out_shape=jax.ShapeDtypeStruct((m, d), F32),
        compiler_params=_params("parallel"),
        name="xattn",
    )(h, g.reshape(1, d), wq, kv, wo)


def kernel(x, mem, positions, ffn1_norm, ffn1_w_gate, ffn1_w_up, ffn1_w_down, mix_norm, ev_w_in, ev_w_out, gla_w_alpha, gla_b_alpha, gla_norm, s5_lambda_re, s5_lambda_im, s5_log_dt, s5_B_re, s5_B_im, s5_C_re, s5_C_im, s5_D, s5_w_glu, s5_b_glu, od_w_in, od_w_out, xa_norm, mem_norm, xa_wq, xa_wk, xa_wv, xa_wo, ffn2_norm, ffn2_w_gate, ffn2_w_up, ffn2_w_down, final_norm):
    b, l, d = x.shape
    n_mem = mem.shape[1]
    depth = ffn1_norm.shape[0]
    tables = _rope_tables(positions, ATT_HEAD_DIM) + _rope_tables(positions, IDX_DIM)
    mem2 = mem.reshape(b * n_mem, d)
    h = x.reshape(b * l, d)

    def seq(a):
        return a.reshape(b, l, a.shape[-1])

    def flat(a):
        return a.reshape(b * l, a.shape[-1])

    for layer in range(depth):
        h = _ffn(h, ffn1_norm[layer], ffn1_w_gate, ffn1_w_up, ffn1_w_down, layer)
        if layer % 2 == 0:
            e = layer // 2
            qk, v, r, la, u = _ev_in(h, mix_norm[layer], ev_w_in[e], gla_w_alpha[e], gla_b_alpha[e])
            o_gla = _gla(seq(qk), seq(v), seq(r), seq(la), gla_norm[e])
            o_s5 = _s5(seq(u), s5_lambda_re[e], s5_lambda_im[e], s5_log_dt[e], s5_B_re[e],
                       s5_B_im[e], s5_C_re[e], s5_C_im[e], s5_D[e], s5_w_glu[e], s5_b_glu[e])
            h = _proj_res(h, [flat(o_gla), flat(o_s5)], ev_w_out, e)
        else:
            o = layer // 2
            q, k, v, qi, ki, wi = _od_in(h, mix_norm[layer], od_w_in[o], tables)
            att = _dsa(q, seq(k), seq(v), seq(qi), seq(ki), seq(wi))
            h = _proj_res(h, [flat(att)], od_w_out, o)
        kv = _norm_proj(mem2, mem_norm, jnp.concatenate([xa_wk[layer], xa_wv[layer]], axis=1),
                        n_mem, BF16)
        h = _xattn(h, xa_norm[layer], xa_wq, kv.reshape(b, n_mem, 2 * d), xa_wo, layer, l)
        last = layer == depth - 1
        h = _ffn(h, ffn2_norm[layer], ffn2_w_gate, ffn2_w_up, ffn2_w_down, layer,
                 final_g=final_norm if last else None)
    return h.reshape(b, l, d)
```

```python
import functools
import math

import jax
import jax.numpy as jnp
from jax import lax
from jax.experimental import pallas as pl
from jax.experimental.pallas import tpu as pltpu

F32 = jnp.float32
BF16 = jnp.bfloat16
I32 = jnp.int32

EPS = 1e-6
ROPE_THETA = 10000.0
GLA_HEADS, GLA_DK, GLA_DV, GLA_RANK, GLA_TAU, GLA_CHUNK = 4, 64, 128, 16, 16.0, 64
S5_GROUP, S5_STATE = 16, 64
ATT_HEADS, ATT_KV_HEADS, ATT_HEAD_DIM = 8, 2, 128
IDX_HEADS, IDX_DIM = 8, 64
TOPK_MAX = 256
XA_HEADS = 4

LANES = 128
VMEM_LIMIT = 56 * 1024 * 1024
ROW_TILE = 512
FFN_CHUNK = 256
GLA_TILE = 256
S5_TILE = 256
S5_BLOCK = 8
S5_SPLIT = 4
DSA_TQ = 256
DSA_TK = 512
SEARCH_MIN_PASSES = 12
NEG = -(2.0 ** 100)
SUBLANES_BF16 = 16

_NT = (((1,), (1,)), ((), ()))
_TN = (((0,), (0,)), ((), ()))


def _params(*sem):
    return pltpu.CompilerParams(dimension_semantics=sem, vmem_limit_bytes=VMEM_LIMIT)


def _const_spec(a):
    if isinstance(a, tuple):
        a, layer = a
        nd = a.ndim - 1
        return pl.BlockSpec((None,) + a.shape[1:], lambda *_: (layer,) + (0,) * nd,
                            pipeline_mode=pl.Buffered(1))
    nd = a.ndim
    return pl.BlockSpec(a.shape, lambda *_: (0,) * nd, pipeline_mode=pl.Buffered(1))


def _rms(x, g):
    return x * lax.rsqrt(jnp.mean(x * x, axis=-1, keepdims=True) + EPS) * g


def _dot(a, b):
    return jnp.dot(a, b, preferred_element_type=F32)


def _row_tiled(body, rows, consts, outs, tm, name):
    m = rows[0].shape[0]
    assert m % tm == 0

    def out_desc(o):
        if len(o) == 2:
            return pl.BlockSpec((tm, o[0]), lambda i: (i, 0)), jax.ShapeDtypeStruct((m, o[0]), o[1])
        return (pl.BlockSpec((o[0], tm, o[1]), lambda i: (0, i, 0)),
                jax.ShapeDtypeStruct((o[0], m, o[1]), o[2]))

    descs = [out_desc(o) for o in outs]
    return pl.pallas_call(
        body,
        grid=(m // tm,),
        in_specs=[pl.BlockSpec((tm, r.shape[1]), lambda i: (i, 0)) for r in rows]
        + [_const_spec(c) for c in consts],
        out_specs=[d[0] for d in descs],
        out_shape=[d[1] for d in descs],
        compiler_params=_params("parallel"),
        name=name,
    )(*rows, *[c[0] if isinstance(c, tuple) else c for c in consts])


def _ffn_body(h_ref, g_ref, wg_ref, wu_ref, wd_ref, *rest, final):
    o_ref = rest[-1]
    x = h_ref[...]
    n = _rms(x, g_ref[...]).astype(BF16)
    acc = jnp.zeros_like(x)
    for c in range(wg_ref.shape[1] // FFN_CHUNK):
        sl = slice(c * FFN_CHUNK, (c + 1) * FFN_CHUNK)
        g = _dot(n, wg_ref[:, sl].astype(BF16))
        u = _dot(n, wu_ref[:, sl].astype(BF16))
        a = (g * jax.nn.sigmoid(g) * u).astype(BF16)
        acc = acc + _dot(a, wd_ref[sl, :].astype(BF16))
    y = x + 0.5 * acc
    if final:
        y = _rms(y, rest[0][...])
    o_ref[...] = y


def _ffn(h, g, wg, wu, wd, layer, final_g=None):
    d = h.shape[1]
    consts = [g.reshape(1, d), (wg, layer), (wu, layer), (wd, layer)]
    if final_g is not None:
        consts.append(final_g.reshape(1, d))
    body = functools.partial(_ffn_body, final=final_g is not None)
    return _row_tiled(body, [h], consts, [(d, F32)], ROW_TILE, "ffn")[0]


def _norm_proj_body(x_ref, g_ref, w_ref, o_ref):
    n = _rms(x_ref[...], g_ref[...]).astype(BF16)
    o_ref[...] = _dot(n, w_ref[...]).astype(o_ref.dtype)


def _norm_proj(x, g, w, tm, out_dtype):
    return _row_tiled(_norm_proj_body, [x], [g.reshape(1, -1), w.astype(BF16)],
                      [(w.shape[1], out_dtype)], tm, "norm_proj")[0]


def _log_sigmoid(x):
    return jnp.minimum(x, 0.0) - jnp.log(1.0 + jnp.exp(-jnp.abs(x)))


def _ev_in_body(h_ref, g_ref, wm_ref, wa_ref, wa2_ref, ba_ref, wu_ref,
                qk_ref, v_ref, r_ref, la_ref, u_ref):
    n = _rms(h_ref[...], g_ref[...]).astype(BF16)
    nqk = qk_ref.shape[1]
    nv = v_ref.shape[1]
    main = _dot(n, wm_ref[...])
    qk_ref[...] = main[:, :nqk]
    v_ref[...] = main[:, nqk:nqk + nv].astype(v_ref.dtype)
    r_ref[...] = main[:, nqk + nv:]
    a_low = _dot(n, wa_ref[...]).astype(BF16)
    alpha = _dot(a_low, wa2_ref[...]) + ba_ref[...]
    la_ref[...] = _log_sigmoid(alpha) / GLA_TAU
    u_ref[...] = _dot(n, wu_ref[...])


def _ev_in(h, g, w_in, w_a2, b_a):
    d = h.shape[1]
    nqk = 2 * GLA_HEADS * GLA_DK
    nv = GLA_HEADS * GLA_DV
    n_main = nqk + 2 * nv
    w_main = w_in[:, :n_main]
    w_a = jnp.pad(w_in[:, n_main:n_main + GLA_RANK], ((0, 0), (0, LANES - GLA_RANK)))
    w_u = w_in[:, n_main + GLA_RANK:]
    w_a2p = jnp.pad(w_a2, ((0, LANES - GLA_RANK), (0, 0)))
    consts = [g.reshape(1, d), w_main.astype(BF16), w_a.astype(BF16), w_a2p.astype(BF16),
              b_a.reshape(1, -1), w_u.astype(BF16)]
    outs = [(nqk, F32), (nv, BF16), (nv, F32), (GLA_HEADS * GLA_DK, F32), (w_u.shape[1], F32)]
    return _row_tiled(_ev_in_body, [h], consts, outs, ROW_TILE, "ev_in")


def _gla_body(qk_ref, v_ref, r_ref, la_ref, gn_ref, o_ref, st_ref):
    tl = qk_ref.shape[0]
    hdk = GLA_HEADS * GLA_DK
    c = GLA_CHUNK

    @pl.when(pl.program_id(1) == 0)
    def _():
        st_ref[...] = jnp.zeros_like(st_ref)

    row = lax.broadcasted_iota(I32, (tl, hdk), 0) % c
    bc = la_ref[...]
    d = 1
    while d < c:
        bc = bc + jnp.where(row >= d, pltpu.roll(bc, d, axis=0), 0.0)
        d *= 2
    q_dec = qk_ref[:, :hdk] * (GLA_DK ** -0.5) * jnp.exp(bc)
    k = qk_ref[:, hdk:]
    k_inv = k * jnp.exp(-bc)
    causal = lax.broadcasted_iota(I32, (c, c), 0) >= lax.broadcasted_iota(I32, (c, c), 1)
    gn = gn_ref[...]
    for j in range(tl // c):
        rs = slice(j * c, (j + 1) * c)
        b_last = bc[j * c + c - 1:j * c + c, :]
        k_end = k[rs] * jnp.exp(b_last - bc[rs])
        dec = jnp.exp(b_last)
        for h in range(GLA_HEADS):
            ks = slice(h * GLA_DK, (h + 1) * GLA_DK)
            vs = slice(h * GLA_DV, (h + 1) * GLA_DV)
            qd = q_dec[rs, ks].astype(BF16)
            vh = v_ref[rs, vs].astype(BF16)
            s = lax.dot_general(qd, k_inv[rs, ks].astype(BF16), _NT, preferred_element_type=F32)
            s = jnp.where(causal, s, 0.0).astype(BF16)
            st = st_ref[h]
            o = _dot(s, vh) + lax.dot_general(qd, st.astype(BF16), _NT, preferred_element_type=F32)
            st_ref[h] = st * dec[:, ks] + lax.dot_general(
                vh, k_end[:, ks].astype(BF16), _TN, preferred_element_type=F32)
            o = o * lax.rsqrt(jnp.mean(o * o, axis=-1, keepdims=True) + EPS) * gn
            r = r_ref[rs, vs]
            o_ref[rs, vs] = (o * (r * jax.nn.sigmoid(r))).astype(o_ref.dtype)


def _gla(qk, v, r, la, gn):
    b, l, _ = qk.shape
    tl = GLA_TILE

    def spec(w):
        return pl.BlockSpec((None, tl, w), lambda bi, t: (bi, t, 0))

    return pl.pallas_call(
        _gla_body,
        grid=(b, l // tl),
        in_specs=[spec(qk.shape[2]), spec(v.shape[2]), spec(r.shape[2]), spec(la.shape[2]),
                  pl.BlockSpec((1, GLA_DV), lambda bi, t: (0, 0))],
        out_specs=spec(v.shape[2]),
        out_shape=jax.ShapeDtypeStruct(v.shape, BF16),
        scratch_shapes=[pltpu.VMEM((GLA_HEADS, GLA_DV, GLA_DK), F32)],
        compiler_params=_params("parallel", "arbitrary"),
        name="gla",
    )(qk, v, r, la, gn.reshape(1, GLA_DV))


def _cmul(ar, ai, br, bi):
    return ar * br - ai * bi, ar * bi + ai * br


def _s5_prep_body(lr_ref, li_ref, ldt_ref, br_ref, bi_ref, bbr_ref, bbi_ref, pr_ref, pi_ref):
    lr = jnp.minimum(lr_ref[...], -1e-4)
    li = li_ref[...]
    dt = jnp.exp(ldt_ref[...])
    mag = jnp.exp(lr * dt)
    ar = mag * jnp.cos(li * dt)
    ai = mag * jnp.sin(li * dt)
    den = lr * lr + li * li
    nr = ar - 1.0
    fr = (nr * lr + ai * li) / den
    fi = (ai * lr - nr * li) / den
    w = br_ref.shape[2]
    for j in range(br_ref.shape[0]):
        sl = slice(j * w, (j + 1) * w)
        b_re, b_im = br_ref[j], bi_ref[j]
        bbr_ref[j] = (fr[:, sl] * b_re - fi[:, sl] * b_im).astype(BF16)
        bbi_ref[j] = (fr[:, sl] * b_im + fi[:, sl] * b_re).astype(BF16)
    pr_ref[0:1, :] = ar
    pi_ref[0:1, :] = ai
    n = 1
    while n < pr_ref.shape[0]:
        sr, si = pr_ref[n - 1:n, :], pi_ref[n - 1:n, :]
        nr_, ni_ = _cmul(pr_ref[0:n, :], pi_ref[0:n, :], sr, si)
        pr_ref[n:2 * n, :] = nr_
        pi_ref[n:2 * n, :] = ni_
        n *= 2


def _s5_body(u_ref, bbr_ref, bbi_ref, pr_ref, pi_ref, cr_ref, ci_ref, d_ref, wg_ref, bg_ref,
             o_ref, sr_ref, si_ref):
    t = u_ref.shape[0]
    w = bbr_ref.shape[2]
    cw = bbr_ref.shape[1]

    @pl.when(pl.program_id(1) == 0)
    def _():
        sr_ref[...] = jnp.zeros_like(sr_ref)
        si_ref[...] = jnp.zeros_like(si_ref)

    u = u_ref[...]
    ub = u.astype(BF16)
    nb = pr_ref.shape[0]
    row = lax.broadcasted_iota(I32, (1, nb, w), 1)
    ys = []
    for j in range(bbr_ref.shape[0]):
        ls = slice(j * w, (j + 1) * w)
        uj = ub[:, j * cw:(j + 1) * cw]
        xr = _dot(uj, bbr_ref[j]).reshape(t // nb, nb, w)
        xi = _dot(uj, bbi_ref[j]).reshape(t // nb, nb, w)
        d = 1
        while d < nb:
            ar = jnp.where(row >= d, pr_ref[d - 1:d, ls], 0.0)
            ai = jnp.where(row >= d, pi_ref[d - 1:d, ls], 0.0)
            dr, di = _cmul(ar, ai, pltpu.roll(xr, d, axis=1), pltpu.roll(xi, d, axis=1))
            xr, xi = xr + dr, xi + di
            d *= 2
        xr, xi = xr.reshape(t, w), xi.reshape(t, w)
        p_r, p_i = pr_ref[:, ls], pi_ref[:, ls]
        c_r, c_i = sr_ref[:, ls], si_ref[:, ls]
        blocks_r, blocks_i = [], []
        for blk in range(t // nb):
            dr, di = _cmul(p_r, p_i, c_r, c_i)
            b_r = xr[blk * nb:(blk + 1) * nb] + dr
            b_i = xi[blk * nb:(blk + 1) * nb] + di
            c_r, c_i = b_r[nb - 1:nb], b_i[nb - 1:nb]
            blocks_r.append(b_r)
            blocks_i.append(b_i)
        sr_ref[:, ls] = c_r
        si_ref[:, ls] = c_i
        xr = jnp.concatenate(blocks_r, axis=0)
        xi = jnp.concatenate(blocks_i, axis=0)
        ys.append(_dot(xr.astype(BF16), cr_ref[j]) - _dot(xi.astype(BF16), ci_ref[j]))
    y = jnp.concatenate(ys, axis=1) + d_ref[...] * u
    y = jax.nn.gelu(y)
    gate = jax.nn.sigmoid(_dot(y.astype(BF16), wg_ref[...]) + bg_ref[...])
    o_ref[...] = (y * gate).astype(o_ref.dtype)


def _block_diag(m, split):
    g, r, c = m.shape
    gs = g // split
    eye = jnp.eye(gs, dtype=m.dtype)
    m = m.reshape(split, gs, r, c)
    out = m[:, :, :, None, :] * eye[None, :, None, :, None]
    return out.reshape(split, gs * r, gs * c)


def _s5(u, lam_re, lam_im, log_dt, b_re, b_im, c_re, c_im, dd, w_glu, b_glu):
    b, l, width = u.shape
    g, p = lam_re.shape
    n_state = g * p
    t = S5_TILE
    br_bd = _block_diag(jnp.swapaxes(b_re, 1, 2), S5_SPLIT)
    bi_bd = _block_diag(jnp.swapaxes(b_im, 1, 2), S5_SPLIT)
    cr_bd = _block_diag(jnp.swapaxes(c_re, 1, 2), S5_SPLIT).astype(BF16)
    ci_bd = _block_diag(jnp.swapaxes(c_im, 1, 2), S5_SPLIT).astype(BF16)
    ldt = jnp.broadcast_to(log_dt[:, None], (g, p)).reshape(1, n_state)
    bbr, bbi, pw_r, pw_i = pl.pallas_call(
        _s5_prep_body,
        out_shape=[jax.ShapeDtypeStruct(br_bd.shape, BF16), jax.ShapeDtypeStruct(br_bd.shape, BF16),
                   jax.ShapeDtypeStruct((S5_BLOCK, n_state), F32),
                   jax.ShapeDtypeStruct((S5_BLOCK, n_state), F32)],
        compiler_params=pltpu.CompilerParams(vmem_limit_bytes=VMEM_LIMIT),
        name="s5_prep",
    )(lam_re.reshape(1, n_state), lam_im.reshape(1, n_state), ldt, br_bd, bi_bd)

    consts = [bbr, bbi, pw_r, pw_i, cr_bd, ci_bd, dd.reshape(1, width), w_glu.astype(BF16),
              b_glu.reshape(1, width)]
    return pl.pallas_call(
        _s5_body,
        grid=(b, l // t),
        in_specs=[pl.BlockSpec((None, t, width), lambda bi_, ti: (bi_, ti, 0))]
        + [_const_spec(c) for c in consts],
        out_specs=pl.BlockSpec((None, t, width), lambda bi_, ti: (bi_, ti, 0)),
        out_shape=jax.ShapeDtypeStruct(u.shape, BF16),
        scratch_shapes=[pltpu.VMEM((1, n_state), F32), pltpu.VMEM((1, n_state), F32)],
        compiler_params=_params("parallel", "arbitrary"),
        name="s5",
    )(u, *consts)


def _rope(x, cos, sin_signed, half):
    lane = lax.broadcasted_iota(I32, x.shape, 1) % (2 * half)
    rot = jnp.where(lane < half, pltpu.roll(x, LANES - half, axis=1), pltpu.roll(x, half, axis=1))
    return x * cos + rot * sin_signed


def _od_in_body(h_ref, ca_ref, sa_ref, ci_ref, si_ref, g_ref, wm_ref, wk_ref, ww_ref,
                q_ref, k_ref, v_ref, qi_ref, ki_ref, wi_ref):
    n = _rms(h_ref[...], g_ref[...]).astype(BF16)
    main = _dot(n, wm_ref[...])
    ca, sa, ci, si = ca_ref[...], sa_ref[...], ci_ref[...], si_ref[...]
    hd = ATT_HEAD_DIM
    nq, nk = q_ref.shape[0] * hd, k_ref.shape[1]
    for h in range(nq // hd):
        sl = slice(h * hd, (h + 1) * hd)
        q_ref[h] = (_rope(main[:, sl], ca, sa, hd // 2) * (hd ** -0.5 * math.log2(math.e))).astype(BF16)
    for h in range(nk // hd):
        k_ref[:, h * hd:(h + 1) * hd] = _rope(
            main[:, nq + h * hd:nq + (h + 1) * hd], ca, sa, hd // 2).astype(BF16)
    v_ref[...] = main[:, nq + nk:nq + 2 * nk].astype(BF16)
    o = nq + 2 * nk
    first = lax.broadcasted_iota(I32, (h_ref.shape[0], LANES), 1) < IDX_DIM
    for s in range(IDX_HEADS * IDX_DIM // LANES):
        pair = _rope(main[:, o + s * LANES:o + (s + 1) * LANES], ci, si, IDX_DIM // 2)
        qi_ref[:, 2 * s * LANES:(2 * s + 1) * LANES] = jnp.where(first, pair, 0.0).astype(BF16)
        qi_ref[:, (2 * s + 1) * LANES:(2 * s + 2) * LANES] = jnp.where(
            first, pltpu.roll(pair, IDX_DIM, axis=1), 0.0).astype(BF16)
    ki_ref[...] = _rope(_dot(n, wk_ref[...]), ci, si, IDX_DIM // 2).astype(BF16)
    wi_ref[...] = _dot(n, ww_ref[...]) * (IDX_HEADS ** -0.5 * IDX_DIM ** -0.5)


def _rope_tables(positions, dim):
    inv = ROPE_THETA ** (-jnp.arange(0, dim, 2, dtype=F32) / dim)
    ang = positions.astype(F32)[..., None] * inv
    cos = jnp.cos(ang)
    sin = jnp.sin(ang)
    reps = LANES // dim
    cos = jnp.tile(jnp.concatenate([cos, cos], -1), (1, 1, reps))
    sin = jnp.tile(jnp.concatenate([-sin, sin], -1), (1, 1, reps))
    return cos.reshape(-1, LANES), sin.reshape(-1, LANES)


def _od_in(h, g, w_in, tables):
    d = h.shape[1]
    nq = ATT_HEADS * ATT_HEAD_DIM
    nk = ATT_KV_HEADS * ATT_HEAD_DIM
    nqi = IDX_HEADS * IDX_DIM
    n_main = nq + 2 * nk + nqi
    w_main = w_in[:, :n_main]
    w_ki = jnp.pad(w_in[:, n_main:n_main + IDX_DIM], ((0, 0), (0, LANES - IDX_DIM)))
    w_wi = jnp.pad(w_in[:, n_main + IDX_DIM:], ((0, 0), (0, LANES - IDX_HEADS)))
    consts = [g.reshape(1, d), w_main.astype(BF16), w_ki.astype(BF16), w_wi.astype(BF16)]
    outs = [(ATT_HEADS, ATT_HEAD_DIM, BF16), (nk, BF16), (nk, BF16), (IDX_HEADS * LANES, BF16),
            (LANES, BF16), (LANES, F32)]
    return _row_tiled(_od_in_body, [h] + list(tables), consts, outs, ROW_TILE, "od_in")


def _sort_key(x):
    bits = lax.bitcast_convert_type(x, I32)
    sign = bits >> 31
    return (bits ^ (sign & jnp.int32(0x7FFFFFFF))) - sign


def _key_value(k):
    sign = k >> 31
    return lax.bitcast_convert_type((k + sign) ^ (sign & jnp.int32(0x7FFFFFFF)), F32)


def _slab_reduce(x, op):
    ways = 4
    parts = [x[r * 8:(r + 1) * 8] for r in range(ways)]
    for r in range(ways, x.shape[0] // 8):
        parts[r % ways] = op(parts[r % ways], x[r * 8:(r + 1) * 8])
    return op(op(parts[0], parts[1]), op(parts[2], parts[3]))


def _sublane_all(x, op):
    for s in (4, 2, 1):
        x = op(x, pltpu.roll(x, s, axis=0))
    return x


def _dsa_body(qi_ref, wi_ref, q_ref, ki_ref, k_ref, vt_ref, o_ref,
              sc_ref, bias_ref, m_ref, acc_ref, *, topk):
    tq = q_ref.shape[1]
    tk = sc_ref.shape[1]
    hd = ATT_HEAD_DIM
    group = ATT_HEADS // ATT_KV_HEADS
    r0 = pl.program_id(1) * tq
    nkb = (r0 + tq + tk - 1) // tk
    qpos = r0 + lax.broadcasted_iota(I32, (1, tq), 1)
    key_ninf = jnp.int32(-2139095040)
    key_lo0 = key_ninf + 1

    def kpos_of(kb):
        return kb * tk + lax.broadcasted_iota(I32, (tk, 1), 0)

    wit = wi_ref[...].T

    def score_block(kb, ext, diagonal):
        kmax, kmin = ext
        c0 = pl.multiple_of(kb * tk, tk)
        kib = ki_ref[pl.ds(c0, tk), :]
        acc = jnp.zeros((tk, tq), F32)
        for h in range(IDX_HEADS):
            lg = lax.dot_general(kib, qi_ref[:, h * LANES:(h + 1) * LANES], _NT,
                                 preferred_element_type=F32)
            acc = acc + jnp.maximum(lg, 0.0) * wit[h:h + 1, :]
        key = _sort_key(acc)
        if diagonal:
            valid = kpos_of(kb) <= qpos
            keys = jnp.where(valid, key, key_ninf)
            lows = jnp.where(valid, key, int_max)
        else:
            keys = lows = key
        sc_ref[kb] = keys
        return (jnp.maximum(kmax, _slab_reduce(keys, jnp.maximum)),
                jnp.minimum(kmin, _slab_reduce(lows, jnp.minimum)))

    int_max = jnp.int32(2147483647)
    n_full = (r0 + 1) // tk
    ext = lax.fori_loop(0, n_full, functools.partial(score_block, diagonal=False),
                        (jnp.full((8, tq), key_ninf, I32), jnp.full((8, tq), int_max, I32)))
    kmax, kmin = lax.fori_loop(n_full, nkb, functools.partial(score_block, diagonal=True), ext)
    kmax = _sublane_all(kmax, jnp.maximum)[0:1]
    kmin = _sublane_all(kmin, jnp.minimum)[0:1]

    def count_ge(cand):
        ways = 4

        def body(kb, parts):
            parts = list(parts)
            for r in range(tk // 8):
                parts[r % ways] = parts[r % ways] + jnp.where(
                    sc_ref[kb, r * 8:(r + 1) * 8, :] >= cand, 1.0, 0.0)
            return tuple(parts)

        parts = lax.fori_loop(0, nkb, body, (jnp.zeros((8, tq), F32),) * ways)
        return _sublane_all((parts[0] + parts[1]) + (parts[2] + parts[3]), jnp.add)[0:1]

    kf = jnp.float32(topk)
    short = (qpos + 1) <= topk

    def search_body(st):
        it, lo, hi, c_lo, c_hi, n_acc, side, streak = st
        mixed = jnp.logical_and(lo < 0, hi > 0)
        w = hi - lo
        active = jnp.logical_not(jnp.logical_or(
            short, jnp.logical_or(c_lo == kf, jnp.logical_and(jnp.logical_not(mixed), w <= 1))))
        gal = lax.shift_left(jnp.int32(1 << 22), jnp.clip(it - 1, 0, 8))
        frac = (c_lo - kf - 0.5) / (c_lo - c_hi)
        d_key = (w.astype(F32) * frac).astype(I32)
        v_lo, v_hi = _key_value(lo), _key_value(hi)
        d_val = _sort_key(v_lo + (v_hi - v_lo) * frac) - lo
        neg = hi <= 0
        d = jnp.where(streak >= 2, w >> 1, jnp.where(neg, d_val, d_key))
        d = jnp.where(jnp.logical_and(n_acc == 0, jnp.logical_not(neg)),
                      w - jnp.minimum(gal, w - 1), d)
        cand = jnp.where(mixed, 0, lo + jnp.clip(d, 1, jnp.maximum(w - 1, 1)))
        cnt = count_ge(cand)
        up = jnp.logical_and(active, cnt >= kf)
        dn = jnp.logical_and(active, cnt < kf)
        lo = jnp.where(up, cand, lo)
        c_lo = jnp.where(up, cnt, c_lo)
        hi = jnp.where(dn, cand, hi)
        c_hi = jnp.where(dn, cnt, c_hi)
        n_acc = n_acc + jnp.where(jnp.logical_and(up, jnp.logical_not(mixed)), 1, 0)
        new_side = jnp.where(up, 1, -1)
        streak = jnp.where(new_side == side, streak + 1, 1)
        return it + 1, lo, hi, c_lo, c_hi, n_acc, new_side, streak

    def pending(st):
        _, lo, hi, c_lo = st[:4]
        mixed = jnp.logical_and(lo < 0, hi > 0)
        settled = jnp.logical_or(
            short, jnp.logical_or(c_lo == kf, jnp.logical_and(jnp.logical_not(mixed), hi - lo <= 1)))
        return jnp.sum(jnp.where(settled, 0.0, 1.0))

    zi = jnp.zeros((1, tq), I32)
    st = (jnp.int32(0), kmin, kmax + 1, (qpos + 1).astype(F32), jnp.zeros((1, tq), F32), zi, zi, zi)
    st = lax.fori_loop(0, SEARCH_MIN_PASSES, lambda _, s: search_body(s), st)
    st, _ = lax.while_loop(
        lambda c: jnp.logical_and(c[0][0] < 200, c[1] > 0.0),
        lambda c: (lambda s: (s, pending(s)))(search_body(search_body(c[0]))),
        (st, pending(st)))
    thr = jnp.where(short, key_lo0, st[1])
    cnt_thr = st[3]

    tied = jnp.logical_and(cnt_thr > kf, jnp.logical_not(short))
    n_tied = jnp.sum(jnp.where(tied, 1.0, 0.0))

    @pl.when(n_tied > 0.0)
    def _():
        need = kf - st[4]
        ltri = jnp.where(lax.broadcasted_iota(I32, (tk, tk), 0) >= lax.broadcasted_iota(I32, (tk, tk), 1),
                         1.0, 0.0).astype(BF16)

        def demote(kb, before):
            keys = sc_ref[kb]
            is_tie = keys == thr
            rank = _dot(ltri, jnp.where(is_tie, 1.0, 0.0).astype(BF16)) + before
            drop = jnp.logical_and(jnp.logical_and(is_tie, rank > need), tied)
            sc_ref[kb] = jnp.where(drop, key_ninf, keys)
            return rank[tk - 1:tk, :]

        lax.fori_loop(0, nkb, demote, jnp.zeros((1, tq), F32))

    def to_bias(kb, carry):
        bias_ref[kb] = jnp.where(sc_ref[kb] >= thr, 0.0, NEG).astype(BF16)
        return carry

    lax.fori_loop(0, nkb, to_bias, 0)

    m_ref[...] = jnp.full_like(m_ref, NEG)
    acc_ref[...] = jnp.zeros_like(acc_ref)
    va = vt_ref.shape[1] // ATT_KV_HEADS

    def attend(kb, carry):
        c0 = pl.multiple_of(kb * tk, tk)
        bias = jnp.concatenate([bias_ref[kb]] * group, axis=1)
        for kv in range(ATT_KV_HEADS):
            kblk = k_ref[pl.ds(c0, tk), kv * hd:(kv + 1) * hd]
            qg = q_ref[kv * group:(kv + 1) * group].reshape(group * tq, hd)
            s = lax.dot_general(kblk, qg, _NT, preferred_element_type=F32).astype(BF16) + bias
            m_old = m_ref[kv:kv + 1, :]
            m_new = jnp.maximum(m_old, jnp.max(s, axis=0, keepdims=True).astype(F32))
            p = jnp.exp2(s - m_new.astype(BF16))
            acc_ref[kv] = jnp.exp2(m_old - m_new) * acc_ref[kv] + _dot(
                vt_ref[kb, kv * va:(kv + 1) * va, :], p)
            m_ref[kv:kv + 1, :] = m_new
        return carry

    lax.fori_loop(0, nkb, attend, 0)
    for h in range(ATT_HEADS):
        kv, cs = h // group, slice((h % group) * tq, (h % group + 1) * tq)
        o_ref[:, h * hd:(h + 1) * hd] = (
            acc_ref[kv, 0:hd, cs] / acc_ref[kv, hd:hd + 1, cs]).T.astype(o_ref.dtype)


def _dsa(q, k, v, qi, ki, wi):
    b, l, nv = v.shape
    nh, _, hd = q.shape
    tq, tk = min(DSA_TQ, l), min(DSA_TK, l)
    topk = min(TOPK_MAX, l // 4)
    group = nh // ATT_KV_HEADS
    vt = jnp.swapaxes(v.reshape(b, l // tk, tk, ATT_KV_HEADS, hd), 2, 4)
    vt = vt.swapaxes(2, 3)
    extra = jnp.zeros((b, l // tk, ATT_KV_HEADS, SUBLANES_BF16, tk), v.dtype).at[:, :, :, 0].set(1.0)
    vt = jnp.concatenate([vt, extra], axis=3)
    va = hd + SUBLANES_BF16
    vt = vt.reshape(b, l // tk, ATT_KV_HEADS * va, tk)

    def qspec(w):
        return pl.BlockSpec((None, tq, w), lambda bi, i: (bi, i, 0))

    def kspec(w):
        return pl.BlockSpec((None, l, w), lambda bi, i: (bi, 0, 0))

    return pl.pallas_call(
        functools.partial(_dsa_body, topk=topk),
        grid=(b, l // tq),
        in_specs=[qspec(qi.shape[2]), qspec(wi.shape[2]),
                  pl.BlockSpec((nh, tq, hd), lambda bi, i: (0, bi * (l // tq) + i, 0)),
                  kspec(ki.shape[2]), kspec(k.shape[2]),
                  pl.BlockSpec((None, l // tk, ATT_KV_HEADS * va, tk), lambda bi, i: (bi, 0, 0, 0))],
        out_specs=qspec(nh * hd),
        out_shape=jax.ShapeDtypeStruct((b, l, nh * hd), BF16),
        scratch_shapes=[pltpu.VMEM((l // tk, tk, tq), I32),
                        pltpu.VMEM((l // tk, tk, tq), BF16),
                        pltpu.VMEM((ATT_KV_HEADS, group * tq), F32),
                        pltpu.VMEM((ATT_KV_HEADS, va, group * tq), F32)],
        compiler_params=_params("parallel", "arbitrary"),
        name="dsa",
    )(qi, wi, q, ki, k, vt)


def _xattn_body(*refs):
    h_ref, (wm_ref, g_ref, wq_ref, kv_ref, wo_ref, o_ref) = refs[0], refs[-6:]
    x = h_ref[...]
    r0 = 0
    for m_ref in refs[1:-6]:
        k = m_ref.shape[1]
        x = x + _dot(m_ref[...].astype(BF16), wm_ref[r0:r0 + k, :].astype(BF16))
        r0 += k
    d = x.shape[1]
    hd = d // XA_HEADS
    n = _rms(x, g_ref[...]).astype(BF16)
    q = (_dot(n, wq_ref[...].astype(BF16)) * (hd ** -0.5)).astype(BF16)
    outs = []
    for h in range(XA_HEADS):
        hs = slice(h * hd, (h + 1) * hd)
        s = lax.dot_general(q[:, hs], kv_ref[:, hs], _NT, preferred_element_type=F32)
        p = jnp.exp(s - jnp.max(s, axis=1, keepdims=True))
        p = p / jnp.sum(p, axis=1, keepdims=True)
        outs.append(_dot(p.astype(BF16), kv_ref[:, d + h * hd:d + (h + 1) * hd]))
    o = jnp.concatenate(outs, axis=1).astype(BF16)
    o_ref[...] = x + _dot(o, wo_ref[...].astype(BF16))


def _xattn(h, mix, w_mix, mix_layer, g, wq, kv, wo, layer, rows_per_batch):
    m, d = h.shape
    tm = ROW_TILE
    per = rows_per_batch // tm

    def rows(a):
        return pl.BlockSpec((tm, a.shape[1]), lambda i: (i, 0))

    return pl.pallas_call(
        _xattn_body,
        grid=(m // tm,),
        in_specs=[rows(h)] + [rows(a) for a in mix]
        + [_const_spec((w_mix, mix_layer)), _const_spec(g.reshape(1, d)), _const_spec((wq, layer)),
           pl.BlockSpec((None,) + kv.shape[1:], lambda i: (i // per, 0, 0)),
           _const_spec((wo, layer))],
        out_specs=rows(h),
        out_shape=jax.ShapeDtypeStruct((m, d), F32),
        compiler_params=_params("parallel"),
        name="xattn",
    )(h, *mix, w_mix, g.reshape(1, d), wq, kv, wo)


def kernel(x, mem, positions, ffn1_norm, ffn1_w_gate, ffn1_w_up, ffn1_w_down, mix_norm, ev_w_in, ev_w_out, gla_w_alpha, gla_b_alpha, gla_norm, s5_lambda_re, s5_lambda_im, s5_log_dt, s5_B_re, s5_B_im, s5_C_re, s5_C_im, s5_D, s5_w_glu, s5_b_glu, od_w_in, od_w_out, xa_norm, mem_norm, xa_wq, xa_wk, xa_wv, xa_wo, ffn2_norm, ffn2_w_gate, ffn2_w_up, ffn2_w_down, final_norm):
    b, l, d = x.shape
    n_mem = mem.shape[1]
    depth = ffn1_norm.shape[0]
    tables = _rope_tables(positions, ATT_HEAD_DIM) + _rope_tables(positions, IDX_DIM)
    mem2 = mem.reshape(b * n_mem, d)
    h = x.reshape(b * l, d)

    def seq(a):
        return a.reshape(b, l, a.shape[-1])

    def flat(a):
        return a.reshape(b * l, a.shape[-1])

    for layer in range(depth):
        h = _ffn(h, ffn1_norm[layer], ffn1_w_gate, ffn1_w_up, ffn1_w_down, layer)
        if layer % 2 == 0:
            e = layer // 2
            qk, v, r, la, u = _ev_in(h, mix_norm[layer], ev_w_in[e], gla_w_alpha[e], gla_b_alpha[e])
            o_gla = _gla(seq(qk), seq(v), seq(r), seq(la), gla_norm[e])
            o_s5 = _s5(seq(u), s5_lambda_re[e], s5_lambda_im[e], s5_log_dt[e], s5_B_re[e],
                       s5_B_im[e], s5_C_re[e], s5_C_im[e], s5_D[e], s5_w_glu[e], s5_b_glu[e])
            mix, w_mix, mix_layer = [flat(o_gla), flat(o_s5)], ev_w_out, e
        else:
            o = layer // 2
            q, k, v, qi, ki, wi = _od_in(h, mix_norm[layer], od_w_in[o], tables)
            att = _dsa(q, seq(k), seq(v), seq(qi), seq(ki), seq(wi))
            mix, w_mix, mix_layer = [flat(att)], od_w_out, o
        kv = _norm_proj(mem2, mem_norm, jnp.concatenate([xa_wk[layer], xa_wv[layer]], axis=1),
                        n_mem, BF16)
        h = _xattn(h, mix, w_mix, mix_layer, xa_norm[layer], xa_wq, kv.reshape(b, n_mem, 2 * d),
                   xa_wo, layer, l)
        last = layer == depth - 1
        h = _ffn(h, ffn2_norm[layer], ffn2_w_gate, ffn2_w_up, ffn2_w_down, layer,
                 final_g=final_norm if last else None)
    return h.reshape(b, l, d)
```
